```python
import math
import jax
import jax.numpy as jnp
from jax import lax
import numpy as np

D_MODEL = 4096
BATCH = 4
SEQ = 4096
DEPTH = 1

CTX_LEN = 256
GRID_W = 64
EPS = 1e-6

N_MOD = 6

DN_HEADS = 16
DN_HEAD_DIM = 128
DN_WIDTH = DN_HEADS * DN_HEAD_DIM
DN_CHUNK = 64
CONV_K = 5

GM_GROUPS = 16
GM_GROUP_DIM = 128
GM_WIDTH = GM_GROUPS * GM_GROUP_DIM
GM_CHUNK = 128
GM_ROW_GROUPS = GM_GROUPS // 2

N_EXPERTS = 32
TOP_K = 4
D_FF_EXPERT = 1536
SWIGLU_LIMIT = 7.0
SWIGLU_ALPHA = 1.702
MOE_BLOCK = 128

COL_K = 0
COL_V = COL_K + DN_WIDTH
COL_DECAY = COL_V + DN_WIDTH
COL_Q = COL_DECAY + 4 * DN_HEADS
COL_Z = COL_Q + DN_WIDTH
COL_GU = COL_Z + DN_WIDTH
COL_GV = COL_GU + GM_WIDTH
COL_MERGE = COL_GV + GM_WIDTH
IN_COLS = COL_MERGE + 2 * D_MODEL
STATE_COLS = COL_Q

kernel_name = "hybrid_deltanet_gmlp_moe_dit_block"


def rms_norm(x, g):
    xf = x.astype(jnp.float32)
    y = xf * lax.rsqrt(jnp.mean(xf * xf, axis=-1, keepdims=True) + EPS)
    return y.astype(x.dtype) * g


def layer_norm(x, g, b):
    xf = x.astype(jnp.float32)
    mu = jnp.mean(xf, axis=-1, keepdims=True)
    var = jnp.mean(jnp.square(xf - mu), axis=-1, keepdims=True)
    return ((xf - mu) * lax.rsqrt(var + EPS)).astype(x.dtype) * g + b


def modulation(cond, w_mod, b_mod):
    m = jax.nn.silu(cond) @ w_mod + b_mod
    return jnp.split(m[:, None, :], N_MOD, axis=-1)


def short_conv(x, w):
    ch = x.shape[-1]
    y = lax.conv_general_dilated(
        x, w[:, None, :].astype(x.dtype), window_strides=(1,),
        padding=[(CONV_K // 2, CONV_K // 2)],
        dimension_numbers=("NWC", "WIO", "NWC"), feature_group_count=ch)
    return jax.nn.silu(y)


def to_heads(t, n_heads):
    b, l, _ = t.shape
    return t.reshape(b, l, n_heads, -1).transpose(0, 2, 1, 3)


def l2_normalize(t):
    tf = t.astype(jnp.float32)
    return tf * lax.rsqrt(jnp.sum(tf * tf, axis=-1, keepdims=True) + EPS)


def decay_and_beta(raw, a_log, dt_bias, d):
    a = raw[..., d * DN_HEADS:(d + 1) * DN_HEADS]
    b = raw[..., (2 + d) * DN_HEADS:(3 + d) * DN_HEADS]
    g = -jnp.exp(a_log[d]) * jax.nn.softplus(a + dt_bias[d])
    beta = jax.nn.sigmoid(b)
    return g.transpose(0, 2, 1), beta.transpose(0, 2, 1)


def chunk_gated_delta(k, v, g, beta, s0, q=None):
    bsz, nh, seq_len, _ = k.shape
    dv = v.shape[-1]
    n = seq_len // DN_CHUNK

    def chunks(t):
        return t.reshape(bsz, nh, n, DN_CHUNK, *t.shape[3:])

    k, v, g, beta = chunks(k), chunks(v), chunks(g), chunks(beta)
    g = jnp.cumsum(g, axis=-1)
    pos = jnp.arange(DN_CHUNK)
    incl = pos[:, None] >= pos[None, :]
    strict = pos[:, None] > pos[None, :]
    diff = g[..., :, None] - g[..., None, :]
    decay = jnp.where(incl, jnp.exp(jnp.where(incl, diff, 0.0)), 0.0)
    kb = k * beta[..., None]
    a = jnp.where(strict, jnp.einsum("bhnid,bhnjd->bhnij", kb, k) * decay, 0.0)
    eye = jnp.eye(DN_CHUNK, dtype=a.dtype)
    tmat = lax.linalg.triangular_solve(eye + a, jnp.broadcast_to(eye, a.shape),
                                       left_side=True, lower=True, unit_diagonal=True)
    u = jnp.einsum("bhnij,bhnjd->bhnid", tmat, v * beta[..., None])
    w = jnp.einsum("bhnij,bhnjd->bhnid", tmat, kb * jnp.exp(g)[..., None])
    g_last = g[..., -1]
    k_end = k * jnp.exp(g_last[..., None] - g)[..., None]
    xs = [u, w, k_end, g_last]
    if q is not None:
        q = chunks(q)
        qk = jnp.where(incl, jnp.einsum("bhnid,bhnjd->bhnij", q, k) * decay, 0.0)
        xs += [q * jnp.exp(g)[..., None], qk]
    xs = tuple(jnp.moveaxis(t, 2, 0) for t in xs)

    def step(s, inp):
        u_c, w_c, ke_c, gl_c = inp[:4]
        v_new = u_c - jnp.einsum("bhid,bhde->bhie", w_c, s)
        s_next = s * jnp.exp(gl_c)[..., None, None] + jnp.einsum("bhid,bhie->bhde", ke_c, v_new)
        if len(inp) == 4:
            return s_next, None
        qd_c, qk_c = inp[4:]
        o_c = jnp.einsum("bhid,bhde->bhie", qd_c, s) + jnp.einsum("bhij,bhje->bhie", qk_c, v_new)
        return s_next, o_c

    s_final, o = lax.scan(step, s0, xs)
    if q is None:
        return None, s_final
    return jnp.moveaxis(o, 0, 2).reshape(bsz, nh, seq_len, dv), s_final


def bidirectional_delta(q, k, v, raw, a_log, dt_bias, s0):
    outs, states = [], []
    for d in range(2):
        g, beta = decay_and_beta(raw, a_log, dt_bias, d)
        qd, kd, vd = q, k, v
        if d == 1:
            kd, vd, g, beta = (jnp.flip(t, axis=2) for t in (kd, vd, g, beta))
            qd = None if q is None else jnp.flip(q, axis=2)
        o, s = chunk_gated_delta(kd, vd, g, beta, s0[d], qd)
        states.append(s)
        if o is not None:
            outs.append(jnp.flip(o, axis=2) if d == 1 else o)
    o = outs[0] + outs[1] if outs else None
    return o, (states[0], states[1])


def chunk_spatial_mix(vg, w_s, b_s):
    b, l, ng, cg = vg.shape
    vc = vg.reshape(b, l // GM_CHUNK, GM_CHUNK, ng, cg)
    s = jnp.einsum("gpq,bnqgc->bnpgc", w_s, vc) + b_s.T[None, None, :, :, None]
    return s.reshape(b, l, ng, cg)


def to_column_major(t):
    b, l = t.shape[:2]
    rows = l // GRID_W
    return t.reshape(b, rows, GRID_W, *t.shape[2:]).swapaxes(1, 2).reshape(b, l, *t.shape[2:])


def to_raster(t):
    b, l = t.shape[:2]
    rows = l // GRID_W
    return t.reshape(b, GRID_W, rows, *t.shape[2:]).swapaxes(1, 2).reshape(b, l, *t.shape[2:])


def token_mixer(h, s0, with_output, on_grid, w_in, conv_w, a_log, dt_bias, onorm_g,
                gm_ln_g, gm_ln_b, gm_ws, gm_bs, w_up_a, w_up_b, w_o):
    bsz, seq_len, _ = h.shape
    if s0 is None:
        zero = jnp.zeros((bsz, DN_HEADS, DN_HEAD_DIM, DN_HEAD_DIM), jnp.float32)
        s0 = (zero, zero)
    if with_output:
        p = h @ w_in
        kvq = short_conv(jnp.concatenate([p[..., COL_K:COL_DECAY], p[..., COL_Q:COL_Z]], axis=-1), conv_w)
        q = l2_normalize(to_heads(kvq[..., 2 * DN_WIDTH:], DN_HEADS)) * (DN_HEAD_DIM ** -0.5)
    else:
        p = h @ w_in[:, :STATE_COLS]
        kvq = short_conv(p[..., COL_K:COL_DECAY], conv_w[:, :2 * DN_WIDTH])
        q = None
    k = l2_normalize(to_heads(kvq[..., :DN_WIDTH], DN_HEADS))
    v = to_heads(kvq[..., DN_WIDTH:2 * DN_WIDTH], DN_HEADS).astype(jnp.float32)
    raw = p[..., COL_DECAY:COL_Q].astype(jnp.float32)
    o, states = bidirectional_delta(q, k, v, raw, a_log, dt_bias, s0)
    if not with_output:
        return None, states
    o = o.transpose(0, 2, 1, 3).astype(h.dtype)
    z = p[..., COL_Z:COL_GU].reshape(bsz, seq_len, DN_HEADS, DN_HEAD_DIM)
    y_a = (rms_norm(o, onorm_g) * jax.nn.silu(z)).reshape(bsz, seq_len, DN_WIDTH)
    gu = jax.nn.gelu(p[..., COL_GU:COL_GV])
    gv = layer_norm(jax.nn.gelu(p[..., COL_GV:COL_MERGE]), gm_ln_g, gm_ln_b)
    gv = gv.reshape(bsz, seq_len, GM_GROUPS, GM_GROUP_DIM)
    if on_grid:
        s_row = chunk_spatial_mix(gv[:, :, :GM_ROW_GROUPS], gm_ws[:GM_ROW_GROUPS], gm_bs[:GM_ROW_GROUPS])
        s_col = to_raster(chunk_spatial_mix(to_column_major(gv[:, :, GM_ROW_GROUPS:]),
                                            gm_ws[GM_ROW_GROUPS:], gm_bs[GM_ROW_GROUPS:]))
        s = jnp.concatenate([s_row, s_col], axis=2)
    else:
        s = chunk_spatial_mix(gv, gm_ws, gm_bs)
    y_b = gu * s.reshape(bsz, seq_len, GM_WIDTH)
    gates = jax.nn.sigmoid(p[..., COL_MERGE:])
    merged = gates[..., :D_MODEL] * (y_a @ w_up_a) + gates[..., D_MODEL:] * (y_b @ w_up_b)
    return merged @ w_o, states


def clamped_swiglu(hu):
    glu, lin = hu[..., :D_FF_EXPERT], hu[..., D_FF_EXPERT:]
    glu = jnp.minimum(glu, SWIGLU_LIMIT)
    lin = jnp.clip(lin, -SWIGLU_LIMIT, SWIGLU_LIMIT)
    return glu * jax.nn.sigmoid(SWIGLU_ALPHA * glu) * (lin + 1.0)


def moe(h, w_router, b_router, w1, b1, w2, b2):
    bsz, seq_len, dm = h.shape
    t = h.reshape(-1, dm)
    n_tok = t.shape[0]
    logits = (t @ w_router + b_router).astype(jnp.float32)
    top_v, top_i = lax.top_k(logits, TOP_K)
    wts = jax.nn.softmax(top_v, axis=-1).astype(t.dtype)
    n_asg = n_tok * TOP_K
    e_flat = top_i.reshape(-1)
    tok_flat = jnp.repeat(jnp.arange(n_tok, dtype=jnp.int32), TOP_K)
    w_flat = wts.reshape(-1)
    order = jnp.argsort(e_flat)
    e_sorted = e_flat[order]
    counts = jnp.bincount(e_flat, length=N_EXPERTS)
    padded = (counts + MOE_BLOCK - 1) // MOE_BLOCK * MOE_BLOCK
    pad_end = jnp.cumsum(padded)
    pad_start = pad_end - padded
    grp_start = jnp.cumsum(counts) - counts
    dest = pad_start[e_sorted] + (jnp.arange(n_asg) - grp_start[e_sorted])
    n_slots = n_asg + N_EXPERTS * MOE_BLOCK
    slot_tok = jnp.zeros((n_slots,), jnp.int32).at[dest].set(tok_flat[order])
    slot_w = jnp.zeros((n_slots,), t.dtype).at[dest].set(w_flat[order])
    n_blk = n_slots // MOE_BLOCK
    blk_start = jnp.arange(n_blk) * MOE_BLOCK
    blk_expert = jnp.minimum(jnp.searchsorted(pad_end, blk_start, side="right"), N_EXPERTS - 1)

    def expert_block(args):
        e, toks, ws = args
        xb = t[toks]
        y = clamped_swiglu(xb @ w1[e] + b1[e]) @ w2[e] + b2[e]
        return y * ws[:, None]

    ys = lax.map(expert_block, (blk_expert, slot_tok.reshape(n_blk, MOE_BLOCK),
                                slot_w.reshape(n_blk, MOE_BLOCK)))
    out = jnp.zeros((n_tok, dm), ys.dtype).at[slot_tok].add(ys.reshape(n_slots, dm))
    return out.reshape(bsz, seq_len, dm).astype(h.dtype)


def setup_inputs(seed: int = 0) -> dict:
    key = jax.random.key(seed)
    ks = jax.random.split(key, 32)
    f32 = jnp.float32
    D = D_MODEL

    def nrm(k, shape, std):
        return jax.random.normal(k, shape, f32) * std

    dt = jnp.exp(jax.random.uniform(ks[10], (DEPTH, 2, DN_HEADS), f32, math.log(1e-3), math.log(1e-1)))
    return {
        "x": nrm(ks[0], (BATCH, SEQ, D), 1.0),
        "c": nrm(ks[1], (BATCH, D), 1.0),
        "ctx": nrm(ks[2], (BATCH, CTX_LEN, D), 1.0),
        "c_ctx": nrm(ks[3], (D,), 1.0),
        "w_mod": nrm(ks[4], (DEPTH, D, N_MOD * D), D ** -0.5),
        "b_mod": nrm(ks[5], (DEPTH, N_MOD * D), 0.02),
        "norm1_g": 1.0 + nrm(ks[6], (DEPTH, D), 0.02),
        "w_in": nrm(ks[7], (DEPTH, D, IN_COLS), D ** -0.5),
        "conv_w": nrm(ks[8], (DEPTH, CONV_K, 3 * DN_WIDTH), CONV_K ** -0.5),
        "a_log": jnp.log(jax.random.uniform(ks[9], (DEPTH, 2, DN_HEADS), f32, 1.0, 16.0)),
        "dt_bias": dt + jnp.log(-jnp.expm1(-dt)),
        "onorm_g": 1.0 + nrm(ks[11], (DEPTH, DN_HEAD_DIM), 0.02),
        "gm_ln_g": 1.0 + nrm(ks[12], (DEPTH, GM_WIDTH), 0.02),
        "gm_ln_b": nrm(ks[13], (DEPTH, GM_WIDTH), 0.02),
        "gm_ws": nrm(ks[14], (DEPTH, GM_GROUPS, GM_CHUNK, GM_CHUNK), GM_CHUNK ** -0.5),
        "gm_bs": 1.0 + nrm(ks[15], (DEPTH, GM_GROUPS, GM_CHUNK), 0.02),
        "w_up_a": nrm(ks[16], (DEPTH, DN_WIDTH, D), DN_WIDTH ** -0.5),
        "w_up_b": nrm(ks[17], (DEPTH, GM_WIDTH, D), GM_WIDTH ** -0.5),
        "w_o": nrm(ks[18], (DEPTH, D, D), D ** -0.5),
        "norm2_g": 1.0 + nrm(ks[19], (DEPTH, D), 0.02),
        "w_router": nrm(ks[20], (DEPTH, D, N_EXPERTS), D ** -0.5),
        "b_router": nrm(ks[21], (DEPTH, N_EXPERTS), 0.01),
        "w1": nrm(ks[22], (DEPTH, N_EXPERTS, D, 2 * D_FF_EXPERT), D ** -0.5),
        "b1": nrm(ks[23], (DEPTH, N_EXPERTS, 2 * D_FF_EXPERT), 0.01),
        "w2": nrm(ks[24], (DEPTH, N_EXPERTS, D_FF_EXPERT, D), D_FF_EXPERT ** -0.5),
        "b2": nrm(ks[25], (DEPTH, N_EXPERTS, D), 0.01),
        "normf_g": 1.0 + nrm(ks[26], (D,), 0.02),
    }


def reference(x, c, ctx, c_ctx, w_mod, b_mod, norm1_g, w_in, conv_w, a_log, dt_bias, onorm_g,
              gm_ln_g, gm_ln_b, gm_ws, gm_bs, w_up_a, w_up_b, w_o, norm2_g, w_router, b_router,
              w1, b1, w2, b2, normf_g):
    for layer in range(DEPTH):
        last = layer == DEPTH - 1
        mix_w = (w_in[layer], conv_w[layer], a_log[layer], dt_bias[layer], onorm_g[layer],
                 gm_ln_g[layer], gm_ln_b[layer], gm_ws[layer], gm_bs[layer],
                 w_up_a[layer], w_up_b[layer], w_o[layer])
        moe_w = (w_router[layer], b_router[layer], w1[layer], b1[layer], w2[layer], b2[layer])
        sh1, sc1, gt1, sh2, sc2, gt2 = modulation(c, w_mod[layer], b_mod[layer])
        csh1, csc1, cgt1, csh2, csc2, cgt2 = modulation(c_ctx[None], w_mod[layer], b_mod[layer])
        hc = rms_norm(ctx, norm1_g[layer]) * (1.0 + csc1) + csh1
        ctx_mix, ctx_states = token_mixer(hc, None, not last, False, *mix_w)
        h = rms_norm(x, norm1_g[layer]) * (1.0 + sc1) + sh1
        lat_mix, _ = token_mixer(h, ctx_states, True, True, *mix_w)
        x = x + gt1 * lat_mix
        h2 = rms_norm(x, norm2_g[layer]) * (1.0 + sc2) + sh2
        x = x + gt2 * moe(h2, *moe_w)
        if not last:
            ctx = ctx + cgt1 * ctx_mix
            hc2 = rms_norm(ctx, norm2_g[layer]) * (1.0 + csc2) + csh2
            ctx = ctx + cgt2 * moe(hc2, *moe_w)
    return rms_norm(x, normf_g)
```

```python
import functools

import jax
import jax.numpy as jnp
from jax import lax
from jax.experimental import pallas as pl
from jax.experimental.pallas import tpu as pltpu

F32 = jnp.float32
BF16 = jnp.bfloat16
I32 = jnp.int32
HIGHEST = lax.Precision.HIGHEST

EPS = 1e-6
N_MOD = 6
GRID_W = 64
HEAD = 128
DN_CHUNK = 64
GM_CHUNK = 128
TOP_K = 4
SWIGLU_LIMIT = 7.0
SWIGLU_ALPHA = 1.702
MOE_ROWS = 256
LANES = 128
SUBLANES = 8
VMEM_LIMIT = 56 * 1024 * 1024

NT_DIMS = (((1,), (1,)), ((), ()))
TN_DIMS = (((0,), (0,)), ((), ()))


def _tile(dim, pref, align=LANES):
    t = min(pref, dim) // align * align
    while t >= align:
        if dim % t == 0:
            return t
        t -= align
    return dim


def _params(*sem):
    return pltpu.CompilerParams(dimension_semantics=sem, vmem_limit_bytes=VMEM_LIMIT)


def _silu(x):
    return x * jax.nn.sigmoid(x)


def _gelu_tanh(x):
    c = 0.7978845608028654
    return 0.5 * x * (1.0 + jnp.tanh(c * (x + 0.044715 * (x * x * x))))


def _mod_kernel(c_ref, w_ref, b_ref, o_ref):
    s = _silu(c_ref[...]).astype(BF16)
    o_ref[...] = jnp.dot(s, w_ref[...].astype(BF16), preferred_element_type=F32) + b_ref[...]


def _modulation(cond, w_mod, b_mod):
    rows, d = cond.shape
    n = w_mod.shape[1]
    tn = _tile(n, 512)
    return pl.pallas_call(
        _mod_kernel,
        grid=(n // tn,),
        in_specs=[pl.BlockSpec((rows, d), lambda j: (0, 0)),
                  pl.BlockSpec((d, tn), lambda j: (0, j)),
                  pl.BlockSpec((1, tn), lambda j: (0, j))],
        out_specs=pl.BlockSpec((rows, tn), lambda j: (0, j)),
        out_shape=jax.ShapeDtypeStruct((rows, n), F32),
        compiler_params=_params("parallel"),
        name="modulation",
    )(cond, w_mod, b_mod.reshape(1, n))


def _rms_mod(x, g, sc, sh):
    y = x * lax.rsqrt(jnp.mean(x * x, axis=-1, keepdims=True) + EPS)
    return (y * g) * (1.0 + sc) + sh


def _normmod_kernel(x_ref, g_ref, sc_ref, sh_ref, o_ref):
    o_ref[0] = _rms_mod(x_ref[0], g_ref[...], sc_ref[0], sh_ref[0]).astype(o_ref.dtype)


def _norm_modulate(x, g, sc, sh, row_of_batch):
    b, l, d = x.shape
    tl = _tile(l, 512, SUBLANES)
    return pl.pallas_call(
        _normmod_kernel,
        grid=(b, l // tl),
        in_specs=[pl.BlockSpec((1, tl, d), lambda i, j: (i, j, 0)),
                  pl.BlockSpec((1, d), lambda i, j: (0, 0)),
                  pl.BlockSpec((1, 1, d), lambda i, j: (row_of_batch(i), 0, 0)),
                  pl.BlockSpec((1, 1, d), lambda i, j: (row_of_batch(i), 0, 0))],
        out_specs=pl.BlockSpec((1, tl, d), lambda i, j: (i, j, 0)),
        out_shape=jax.ShapeDtypeStruct((b, l, d), BF16),
        compiler_params=_params("parallel", "parallel"),
        name="norm_modulate",
    )(x, g.reshape(1, d), sc, sh)


def _mm_kernel(a_ref, w_ref, o_ref):
    o_ref[...] = jnp.dot(a_ref[...], w_ref[...], preferred_element_type=F32).astype(o_ref.dtype)


def _matmul(a, w, out_dtype, tm_pref=1024, tn_pref=1024):
    m, k = a.shape
    n = w.shape[1]
    tm = _tile(m, tm_pref, SUBLANES)
    tn = _tile(n, tn_pref)
    return pl.pallas_call(
        _mm_kernel,
        grid=(m // tm, n // tn),
        in_specs=[pl.BlockSpec((tm, k), lambda i, j: (i, 0)),
                  pl.BlockSpec((k, tn), lambda i, j: (0, j))],
        out_specs=pl.BlockSpec((tm, tn), lambda i, j: (i, j)),
        out_shape=jax.ShapeDtypeStruct((m, n), out_dtype),
        compiler_params=_params("parallel", "parallel"),
        name="matmul",
    )(a, w)


def _conv_kernel(prev_ref, cur_ref, next_ref, w_ref, o_ref, *, tl, n_t, tc, k_tiles, conv_k):
    i = pl.program_id(1)
    j = pl.program_id(2)
    prev = jnp.where(i > 0, prev_ref[0], 0.0)
    nxt = jnp.where(i < n_t - 1, next_ref[0], 0.0)
    xx = jnp.concatenate([prev, cur_ref[0], nxt], axis=0)
    n = tl + 2 * SUBLANES
    acc = None
    for jj in range(conv_k):
        s = jj - conv_k // 2
        shifted = xx if s == 0 else pltpu.roll(xx, (-s) % n, 0)
        term = shifted[SUBLANES:SUBLANES + tl] * w_ref[jj:jj + 1, :]
        acc = term if acc is None else acc + term
    y = _silu(acc)
    is_k = j < k_tiles
    is_q = j >= 2 * k_tiles
    outs = []
    for hh in range(tc // HEAD):
        yh = y[:, hh * HEAD:(hh + 1) * HEAD]
        inv = lax.rsqrt(jnp.sum(yh * yh, axis=-1, keepdims=True) + EPS)
        scale = jnp.where(is_k, inv, jnp.where(is_q, inv * (HEAD ** -0.5), 1.0))
        outs.append(yh * scale)
    o_ref[0] = jnp.concatenate(outs, axis=-1)


def _short_conv(p, conv_w, width, n_sections):
    b, l, _ = p.shape
    conv_k = conv_w.shape[0]
    c = n_sections * width
    tc = _tile(width, 512)
    tl = _tile(l, 512, SUBLANES)
    n_t = l // tl
    sub = tl // SUBLANES
    kern = functools.partial(_conv_kernel, tl=tl, n_t=n_t, tc=tc, k_tiles=width // tc, conv_k=conv_k)
    return pl.pallas_call(
        kern,
        grid=(b, n_t, c // tc),
        in_specs=[
            pl.BlockSpec((1, SUBLANES, tc), lambda bi, i, j: (bi, jnp.maximum(i * sub - 1, 0), j)),
            pl.BlockSpec((1, tl, tc), lambda bi, i, j: (bi, i, j)),
            pl.BlockSpec((1, SUBLANES, tc), lambda bi, i, j: (bi, jnp.minimum((i + 1) * sub, l // SUBLANES - 1), j)),
            pl.BlockSpec((conv_k, tc), lambda bi, i, j: (0, j)),
        ],
        out_specs=pl.BlockSpec((1, tl, tc), lambda bi, i, j: (bi, i, j)),
        out_shape=jax.ShapeDtypeStruct((b, l, c), F32),
        compiler_params=_params("parallel", "parallel", "parallel"),
        name="short_conv",
    )(p, p, p, conv_w[:, :c])


def _decay_kernel(raw_ref, alog_ref, dtb_ref, cols_ref, dmat_ref, *, n_heads):
    h = n_heads
    c = DN_CHUNK
    raw = raw_ref[0]
    x = raw + dtb_ref[...]
    softplus = jnp.maximum(x, 0.0) + jnp.log1p(jnp.exp(-jnp.abs(x)))
    gs = -jnp.exp(alog_ref[...]) * softplus
    beta = jax.nn.sigmoid(raw)
    ii = lax.broadcasted_iota(I32, (c, c), 0)
    jj = lax.broadcasted_iota(I32, (c, c), 1)
    tri = [jj <= ii, jj >= ii]
    trif = [t.astype(F32) for t in tri]
    eye = (lax.broadcasted_iota(I32, (LANES, LANES), 0) == lax.broadcasted_iota(I32, (LANES, LANES), 1)).astype(F32)
    gs_t = lax.dot_general(eye, gs, NT_DIMS, precision=HIGHEST, preferred_element_type=F32)
    lane = lax.broadcasted_iota(I32, (c, LANES), 1)
    g_dir = [jnp.dot(trif[d], gs, precision=HIGHEST, preferred_element_type=F32) for d in range(2)]
    g_cum = jnp.where(lane < h, g_dir[0], g_dir[1])
    g_last = jnp.where(lane < h, g_dir[0][c - 1:c, :], g_dir[1][0:1, :])
    cols_ref[0] = jnp.concatenate([beta, jnp.exp(g_cum), jnp.exp(g_last - g_cum), jnp.exp(g_last)], axis=-1)
    for d in range(2):
        g_row = lax.dot_general(gs_t, trif[d], NT_DIMS, precision=HIGHEST, preferred_element_type=F32)
        mats = []
        for hh in range(h):
            col = d * h + hh
            diff = g_dir[d][:, col:col + 1] - g_row[col:col + 1, :]
            mats.append(jnp.where(tri[d], jnp.exp(jnp.where(tri[d], diff, 0.0)), 0.0))
        dmat_ref[0, d] = jnp.concatenate(mats, axis=-1)


def _decay_prep(raw, a_log, dt_bias, n_heads):
    b, l, _ = raw.shape
    h = n_heads
    pad = LANES - 2 * h
    alog = jnp.concatenate([a_log.reshape(1, 2 * h), jnp.zeros((1, pad), F32)], axis=-1)
    dtb = jnp.concatenate([dt_bias.reshape(1, 2 * h), jnp.zeros((1, pad), F32)], axis=-1)
    n = l // DN_CHUNK
    return pl.pallas_call(
        functools.partial(_decay_kernel, n_heads=h),
        grid=(b, n),
        in_specs=[pl.BlockSpec((1, DN_CHUNK, LANES), lambda i, j: (i, j, 0)),
                  pl.BlockSpec((1, LANES), lambda i, j: (0, 0)),
                  pl.BlockSpec((1, LANES), lambda i, j: (0, 0))],
        out_specs=[pl.BlockSpec((1, DN_CHUNK, 4 * LANES), lambda i, j: (i, j, 0)),
                   pl.BlockSpec((1, 2, DN_CHUNK, h * DN_CHUNK), lambda i, j: (i, 0, j, 0))],
        out_shape=[jax.ShapeDtypeStruct((b, l, 4 * LANES), F32),
                   jax.ShapeDtypeStruct((b, 2, l, h * DN_CHUNK), F32)],
        compiler_params=_params("parallel", "parallel"),
        name="decay_prep",
    )(raw, alog, dtb)


def _dot32(a, b):
    return jnp.dot(a, b, precision=HIGHEST, preferred_element_type=F32)


def _dot16(a, b):
    return jnp.dot(a.astype(BF16), b.astype(BF16), preferred_element_type=F32)


def _unit_triangular_inverse(a, ii, jj):
    eye = (ii == jj).astype(F32)
    a0 = jnp.where((ii // 8) == (jj // 8), a, 0.0)
    a2 = _dot32(a0, a0)
    a4 = _dot32(a2, a2)
    p = eye - a0
    p = p + _dot32(p, a2)
    p = p + _dot32(p, a4)
    for s in (8, 16, 32):
        off = jnp.where(((ii // (2 * s)) == (jj // (2 * s))) & ((ii // s) != (jj // s)), a, 0.0)
        p = p - _dot32(p, _dot32(off, p))
    return p


def _delta_kernel(*refs, hb, n_heads, with_q):
    if with_q:
        (kf_ref, vf_ref, qf_ref, kb_ref, vb_ref, qb_ref, df_ref, db_ref, cf_ref, cb_ref, s0_ref,
         of_ref, ob_ref, sfin_ref, s_ref) = refs
        q_refs = (qf_ref, qb_ref)
        o_refs = (of_ref, ob_ref)
    else:
        (kf_ref, vf_ref, kb_ref, vb_ref, df_ref, db_ref, cf_ref, cb_ref, s0_ref, sfin_ref, s_ref) = refs
    k_refs = (kf_ref, kb_ref)
    v_refs = (vf_ref, vb_ref)
    d_refs = (df_ref, db_ref)
    c_refs = (cf_ref, cb_ref)
    hg = pl.program_id(1)
    n = pl.program_id(2)
    c = DN_CHUNK

    @pl.when(n == 0)
    def _():
        s_ref[...] = s0_ref[0]

    ii = lax.broadcasted_iota(I32, (c, c), 0)
    jj = lax.broadcasted_iota(I32, (c, c), 1)
    lane = lax.broadcasted_iota(I32, (c, LANES), 1)

    def column(x, idx):
        return jnp.sum(jnp.where(lane == idx, x, 0.0), axis=-1, keepdims=True)

    for d in range(2):
        strict = (jj < ii) if d == 0 else (jj > ii)
        cols = c_refs[d][0]
        outs = []
        for hh in range(hb):
            head = hg * hb + hh
            beta = column(cols[:, 0:LANES], (2 + d) * n_heads + head)
            eg = column(cols[:, LANES:2 * LANES], d * n_heads + head)
            ekend = column(cols[:, 2 * LANES:3 * LANES], d * n_heads + head)
            eglast = column(cols[:, 3 * LANES:4 * LANES], d * n_heads + head)[0:1, :]
            k = k_refs[d][0][:, hh * HEAD:(hh + 1) * HEAD]
            v = v_refs[d][0][:, hh * HEAD:(hh + 1) * HEAD]
            dm = d_refs[d][0, 0][:, hh * c:(hh + 1) * c]
            kb = k * beta
            kk = lax.dot_general(kb, k, NT_DIMS, precision=HIGHEST, preferred_element_type=F32)
            a = jnp.where(strict, kk * dm, 0.0)
            t = _unit_triangular_inverse(a, ii, jj)
            u = _dot32(t, v * beta)
            w = _dot32(t, kb * eg)
            s = s_ref[d, hh]
            v_new = u - _dot16(w, s)
            if with_q:
                q = q_refs[d][0][:, hh * HEAD:(hh + 1) * HEAD]
                qk = lax.dot_general(q, k, NT_DIMS, precision=HIGHEST, preferred_element_type=F32) * dm
                outs.append(_dot16(q * eg, s) + _dot16(qk, v_new))
            ke = (k * ekend).astype(BF16)
            s_ref[d, hh] = s * eglast + lax.dot_general(ke, v_new.astype(BF16), TN_DIMS, preferred_element_type=F32)
        if with_q:
            o_refs[d][0] = jnp.concatenate(outs, axis=-1)

    @pl.when(n == pl.num_programs(2) - 1)
    def _():
        sfin_ref[0] = s_ref[...]


def _delta_scan(kvq, cols, dmat, s0, n_heads, with_q):
    b, l, _ = kvq.shape
    h = n_heads
    hb = 2 if h % 2 == 0 else 1
    c = DN_CHUNK
    n = l // c
    wblk = h // hb

    def sec(section, rev):
        def imap(bi, g, j):
            return (bi, (n - 1 - j) if rev else j, section * wblk + g)
        return pl.BlockSpec((1, c, hb * HEAD), imap)

    def dspec(d):
        return pl.BlockSpec((1, 1, c, hb * c), lambda bi, g, j: (bi, d, (n - 1 - j) if d else j, g))

    def cspec(d):
        return pl.BlockSpec((1, c, 4 * LANES), lambda bi, g, j: (bi, (n - 1 - j) if d else j, 0))

    state_spec = pl.BlockSpec((1, 2, hb, HEAD, HEAD), lambda bi, g, j: (bi, 0, g, 0, 0))
    n_sec = 3 if with_q else 2
    in_specs = [sec(s, False) for s in range(n_sec)] + [sec(s, True) for s in range(n_sec)]
    in_specs += [dspec(0), dspec(1), cspec(0), cspec(1), state_spec]
    args = [kvq] * (2 * n_sec) + [dmat, dmat, cols, cols, s0]
    out_specs = [state_spec]
    out_shape = [jax.ShapeDtypeStruct((b, 2, h, HEAD, HEAD), F32)]
    if with_q:
        ospec = [pl.BlockSpec((1, c, hb * HEAD), lambda bi, g, j: (bi, j, g)),
                 pl.BlockSpec((1, c, hb * HEAD), lambda bi, g, j: (bi, n - 1 - j, g))]
        out_specs = ospec + out_specs
        out_shape = [jax.ShapeDtypeStruct((b, l, h * HEAD), F32)] * 2 + out_shape
    res = pl.pallas_call(
        functools.partial(_delta_kernel, hb=hb, n_heads=h, with_q=with_q),
        grid=(b, wblk, n),
        in_specs=in_specs,
        out_specs=out_specs,
        out_shape=out_shape,
        scratch_shapes=[pltpu.VMEM((2, hb, HEAD, HEAD), F32)],
        compiler_params=_params("parallel", "parallel", "arbitrary"),
        name="delta_scan_q" if with_q else "delta_scan_state",
    )(*args)
    if with_q:
        return res[0], res[1], res[2]
    return None, None, res[0]


def _gated_norm_kernel(of_ref, ob_ref, z_ref, g_ref, o_ref):
    o = of_ref[0] + ob_ref[0]
    z = z_ref[0]
    outs = []
    for hh in range(o.shape[-1] // HEAD):
        oh = o[:, hh * HEAD:(hh + 1) * HEAD]
        y = oh * lax.rsqrt(jnp.mean(oh * oh, axis=-1, keepdims=True) + EPS) * g_ref[...]
        outs.append(y * _silu(z[:, hh * HEAD:(hh + 1) * HEAD]))
    o_ref[0] = jnp.concatenate(outs, axis=-1).astype(o_ref.dtype)


def _gated_norm(o_f, o_b, p, z_col, onorm_g):
    b, l, w = o_f.shape
    tl = _tile(l, 512, SUBLANES)
    tc = _tile(w, 512)
    zb = z_col // tc
    return pl.pallas_call(
        _gated_norm_kernel,
        grid=(b, l // tl, w // tc),
        in_specs=[pl.BlockSpec((1, tl, tc), lambda i, j, k: (i, j, k)),
                  pl.BlockSpec((1, tl, tc), lambda i, j, k: (i, j, k)),
                  pl.BlockSpec((1, tl, tc), lambda i, j, k: (i, j, zb + k)),
                  pl.BlockSpec((1, HEAD), lambda i, j, k: (0, 0))],
        out_specs=pl.BlockSpec((1, tl, tc), lambda i, j, k: (i, j, k)),
        out_shape=jax.ShapeDtypeStruct((b, l, w), BF16),
        compiler_params=_params("parallel", "parallel", "parallel"),
        name="gated_norm",
    )(o_f, o_b, p, onorm_g.reshape(1, HEAD))


def _gelu_ln_kernel(x_ref, g_ref, b_ref, o_ref):
    x = _gelu_tanh(x_ref[0])
    mu = jnp.mean(x, axis=-1, keepdims=True)
    xc = x - mu
    var = jnp.mean(xc * xc, axis=-1, keepdims=True)
    o_ref[0] = (xc * lax.rsqrt(var + EPS)) * g_ref[...] + b_ref[...]


def _gelu_layernorm(p, col, width, g, bias):
    b, l, _ = p.shape
    tl = _tile(l, 256, SUBLANES)
    cb = col // width
    return pl.pallas_call(
        _gelu_ln_kernel,
        grid=(b, l // tl),
        in_specs=[pl.BlockSpec((1, tl, width), lambda i, j: (i, j, cb)),
                  pl.BlockSpec((1, width), lambda i, j: (0, 0)),
                  pl.BlockSpec((1, width), lambda i, j: (0, 0))],
        out_specs=pl.BlockSpec((1, tl, width), lambda i, j: (i, j, 0)),
        out_shape=jax.ShapeDtypeStruct((b, l, width), F32),
        compiler_params=_params("parallel", "parallel"),
        name="gelu_layernorm",
    )(p, g.reshape(1, width), bias.reshape(1, width))


def _spatial_kernel(gv_ref, gu_ref, ws_ref, bs_ref, o_ref, *, seq, row_groups, grid_w):
    g = pl.program_id(1)
    w = ws_ref[0].astype(BF16)
    bias = bs_ref[0]
    n_chunks = seq // GM_CHUNK
    rows = seq // grid_w
    cols_per_chunk = GM_CHUNK // rows

    @pl.when(g < row_groups)
    def _():
        for n in range(n_chunks):
            sl = pl.ds(n * GM_CHUNK, GM_CHUNK)
            s = jnp.dot(w, gv_ref[0, sl, :].astype(BF16), preferred_element_type=F32) + bias
            o_ref[0, sl, :] = (_gelu_tanh(gu_ref[0, sl, :]) * s).astype(o_ref.dtype)

    @pl.when(g >= row_groups)
    def _():
        for n in range(n_chunks):
            sls = [pl.ds(n * cols_per_chunk + cc, rows, stride=grid_w) for cc in range(cols_per_chunk)]
            v = jnp.concatenate([gv_ref[0, sl, :] for sl in sls], axis=0)
            s = jnp.dot(w, v.astype(BF16), preferred_element_type=F32) + bias
            for cc, sl in enumerate(sls):
                o_ref[0, sl, :] = (_gelu_tanh(gu_ref[0, sl, :]) * s[cc * rows:(cc + 1) * rows]).astype(o_ref.dtype)


def _spatial_gate(gvn, p, gu_col, gm_ws, gm_bs):
    b, l, width = gvn.shape
    groups = gm_ws.shape[0]
    gub = gu_col // HEAD
    kern = functools.partial(_spatial_kernel, seq=l, row_groups=groups // 2, grid_w=GRID_W)
    return pl.pallas_call(
        kern,
        grid=(b, groups),
        in_specs=[pl.BlockSpec((1, l, HEAD), lambda i, g: (i, 0, g)),
                  pl.BlockSpec((1, l, HEAD), lambda i, g: (i, 0, gub + g)),
                  pl.BlockSpec((1, GM_CHUNK, GM_CHUNK), lambda i, g: (g, 0, 0)),
                  pl.BlockSpec((1, GM_CHUNK, 1), lambda i, g: (g, 0, 0))],
        out_specs=pl.BlockSpec((1, l, HEAD), lambda i, g: (i, 0, g)),
        out_shape=jax.ShapeDtypeStruct((b, l, width), F32),
        compiler_params=_params("parallel", "parallel"),
        name="spatial_gate",
    )(gvn, p, gm_ws, gm_bs.reshape(groups, GM_CHUNK, 1))


def _merge_kernel(ya_ref, yb_ref, wa_ref, wb_ref, ga_ref, gb_ref, o_ref):
    pa = jnp.dot(ya_ref[...], wa_ref[...], preferred_element_type=F32)
    pb = jnp.dot(yb_ref[...].astype(BF16), wb_ref[...], preferred_element_type=F32)
    o_ref[...] = (jax.nn.sigmoid(ga_ref[...]) * pa + jax.nn.sigmoid(gb_ref[...]) * pb).astype(o_ref.dtype)


def _merge(ya, yb, wa, wb, p2d, merge_col):
    m, ka = ya.shape
    kb = yb.shape[1]
    d = wa.shape[1]
    tm = _tile(m, 512, SUBLANES)
    tn = _tile(d, 1024)
    ca = merge_col // tn
    cb = (merge_col + d) // tn
    return pl.pallas_call(
        _merge_kernel,
        grid=(m // tm, d // tn),
        in_specs=[pl.BlockSpec((tm, ka), lambda i, j: (i, 0)),
                  pl.BlockSpec((tm, kb), lambda i, j: (i, 0)),
                  pl.BlockSpec((ka, tn), lambda i, j: (0, j)),
                  pl.BlockSpec((kb, tn), lambda i, j: (0, j)),
                  pl.BlockSpec((tm, tn), lambda i, j: (i, ca + j)),
                  pl.BlockSpec((tm, tn), lambda i, j: (i, cb + j))],
        out_specs=pl.BlockSpec((tm, tn), lambda i, j: (i, j)),
        out_shape=jax.ShapeDtypeStruct((m, d), BF16),
        compiler_params=_params("parallel", "parallel"),
        name="merge",
    )(ya, yb, wa, wb, p2d, p2d)


def _oproj_kernel(a_ref, w_ref, x_ref, gt_ref, o_ref):
    y = jnp.dot(a_ref[...], w_ref[...], preferred_element_type=F32)
    o_ref[...] = x_ref[...] + gt_ref[0] * y


def _out_proj_residual(a, w, x2d, gate, seq):
    m, k = a.shape
    d = w.shape[1]
    tm = _tile(seq, 1024, SUBLANES)
    tn = _tile(d, 1024)
    per_batch = seq // tm
    return pl.pallas_call(
        _oproj_kernel,
        grid=(m // tm, d // tn),
        in_specs=[pl.BlockSpec((tm, k), lambda i, j: (i, 0)),
                  pl.BlockSpec((k, tn), lambda i, j: (0, j)),
                  pl.BlockSpec((tm, tn), lambda i, j: (i, j)),
                  pl.BlockSpec((1, 1, tn), lambda i, j: (i // per_batch, 0, j))],
        out_specs=pl.BlockSpec((tm, tn), lambda i, j: (i, j)),
        out_shape=jax.ShapeDtypeStruct((m, d), F32),
        compiler_params=_params("parallel", "parallel"),
        name="out_proj",
    )(a, w, x2d, gate)


def _router_kernel(x_ref, g_ref, sc_ref, sh_ref, wr_ref, br_ref, h_ref, ti_ref, tw_ref, rk_ref, cnt_ref, carry,
                   *, n_exp, tm):
    i = pl.program_id(0)

    @pl.when(i == 0)
    def _():
        carry[...] = jnp.zeros_like(carry)

    h2 = _rms_mod(x_ref[...], g_ref[...], sc_ref[0], sh_ref[0])
    h_ref[...] = h2
    logits = jnp.dot(h2, wr_ref[...], precision=HIGHEST, preferred_element_type=F32) + br_ref[...]
    lane = lax.broadcasted_iota(I32, (tm, n_exp), 1)
    work = logits
    member = jnp.zeros((tm, n_exp), F32)
    vals, ids = [], []
    for _ in range(TOP_K):
        mx = jnp.max(work, axis=-1, keepdims=True)
        idx = jnp.min(jnp.where(work == mx, lane, n_exp), axis=-1, keepdims=True)
        hit = lane == idx
        vals.append(mx)
        ids.append(idx)
        work = jnp.where(hit, -jnp.inf, work)
        member = member + hit.astype(F32)
    exps = [jnp.exp(v - vals[0]) for v in vals]
    den = exps[0]
    for e in exps[1:]:
        den = den + e
    rr = lax.broadcasted_iota(I32, (tm, tm), 0)
    cc = lax.broadcasted_iota(I32, (tm, tm), 1)
    below = (cc < rr).astype(BF16)
    rank_all = jnp.dot(below, member.astype(BF16), preferred_element_type=F32) + carry[...]
    out_lane = lax.broadcasted_iota(I32, (tm, LANES), 1)
    ti = jnp.zeros((tm, LANES), I32)
    tw = jnp.zeros((tm, LANES), F32)
    rk = jnp.zeros((tm, LANES), F32)
    for k in range(TOP_K):
        rank_k = jnp.sum(jnp.where(lane == ids[k], rank_all, 0.0), axis=-1, keepdims=True)
        ti = jnp.where(out_lane == k, ids[k], ti)
        tw = jnp.where(out_lane == k, exps[k] / den, tw)
        rk = jnp.where(out_lane == k, rank_k, rk)
    ti_ref[...] = ti
    tw_ref[...] = tw
    rk_ref[...] = rk.astype(I32)
    carry[...] = carry[...] + jnp.sum(member, axis=0, keepdims=True)
    cnt_ref[...] = carry[...]


def _router(x2d, g, sc, sh, w_router, b_router, seq):
    m, d = x2d.shape
    n_exp = w_router.shape[1]
    tm = _tile(seq, 256, SUBLANES)
    per_batch = seq // tm
    kern = functools.partial(_router_kernel, n_exp=n_exp, tm=tm)
    row = lambda i: (i, 0)
    return pl.pallas_call(
        kern,
        grid=(m // tm,),
        in_specs=[pl.BlockSpec((tm, d), row),
                  pl.BlockSpec((1, d), lambda i: (0, 0)),
                  pl.BlockSpec((1, 1, d), lambda i: (i // per_batch, 0, 0)),
                  pl.BlockSpec((1, 1, d), lambda i: (i // per_batch, 0, 0)),
                  pl.BlockSpec((d, n_exp), lambda i: (0, 0)),
                  pl.BlockSpec((1, n_exp), lambda i: (0, 0))],
        out_specs=[pl.BlockSpec((tm, d), row),
                   pl.BlockSpec((tm, LANES), row),
                   pl.BlockSpec((tm, LANES), row),
                   pl.BlockSpec((tm, LANES), row),
                   pl.BlockSpec((1, n_exp), lambda i: (0, 0))],
        out_shape=[jax.ShapeDtypeStruct((m, d), F32),
                   jax.ShapeDtypeStruct((m, LANES), I32),
                   jax.ShapeDtypeStruct((m, LANES), F32),
                   jax.ShapeDtypeStruct((m, LANES), I32),
                   jax.ShapeDtypeStruct((1, n_exp), F32)],
        scratch_shapes=[pltpu.VMEM((1, n_exp), F32)],
        compiler_params=_params("arbitrary"),
        name="router",
    )(x2d, g.reshape(1, d), sc, sh, w_router, b_router.reshape(1, n_exp))


def _dispatch_kernel(tok_ref, h_hbm, xs_hbm, sem, *, rows):
    base = pl.program_id(0) * rows

    def row_copy(i):
        return pltpu.make_async_copy(h_hbm.at[pl.ds(tok_ref[i], 1)], xs_hbm.at[pl.ds(base + i, 1)], sem)

    def start(i, carry):
        row_copy(i).start()
        return carry

    def wait(i, carry):
        row_copy(i).wait()
        return carry

    lax.fori_loop(0, rows, start, 0)
    lax.fori_loop(0, rows, wait, 0)


def _dispatch(h2, slot_tok):
    n_slots = slot_tok.shape[0]
    d = h2.shape[1]
    rows = MOE_ROWS
    return pl.pallas_call(
        functools.partial(_dispatch_kernel, rows=rows),
        grid=(n_slots // rows,),
        in_specs=[pl.BlockSpec((rows,), lambda i: (i,), memory_space=pltpu.SMEM),
                  pl.BlockSpec(memory_space=pl.ANY)],
        out_specs=pl.BlockSpec(memory_space=pl.ANY),
        out_shape=jax.ShapeDtypeStruct((n_slots, d), h2.dtype),
        scratch_shapes=[pltpu.SemaphoreType.DMA(())],
        compiler_params=pltpu.CompilerParams(dimension_semantics=("arbitrary",), has_side_effects=True),
        name="dispatch",
    )(slot_tok, h2)


def _ffn1_kernel(be_ref, xs_ref, wg_ref, wl_ref, bg_ref, bl_ref, o_ref):
    del be_ref
    x = xs_ref[...].astype(BF16)
    glu = jnp.dot(x, wg_ref[0], preferred_element_type=F32) + bg_ref[0]
    lin = jnp.dot(x, wl_ref[0], preferred_element_type=F32) + bl_ref[0]
    glu = jnp.minimum(glu, SWIGLU_LIMIT)
    lin = jnp.clip(lin, -SWIGLU_LIMIT, SWIGLU_LIMIT)
    o_ref[...] = (glu * jax.nn.sigmoid(SWIGLU_ALPHA * glu) * (lin + 1.0)).astype(o_ref.dtype)


def _ffn1(xs, w1, b1, blk_expert):
    n_slots, d = xs.shape
    n_exp, _, two_f = w1.shape
    f = two_f // 2
    rows = MOE_ROWS
    tn = _tile(f, 512)
    nj = f // tn
    grid_spec = pltpu.PrefetchScalarGridSpec(
        num_scalar_prefetch=1,
        grid=(nj, n_slots // rows),
        in_specs=[pl.BlockSpec((rows, d), lambda j, i, be: (i, 0)),
                  pl.BlockSpec((1, d, tn), lambda j, i, be: (be[i], 0, j)),
                  pl.BlockSpec((1, d, tn), lambda j, i, be: (be[i], 0, nj + j)),
                  pl.BlockSpec((1, 1, tn), lambda j, i, be: (be[i], 0, j)),
                  pl.BlockSpec((1, 1, tn), lambda j, i, be: (be[i], 0, nj + j))],
        out_specs=pl.BlockSpec((rows, tn), lambda j, i, be: (i, j)),
    )
    return pl.pallas_call(
        _ffn1_kernel,
        grid_spec=grid_spec,
        out_shape=jax.ShapeDtypeStruct((n_slots, f), BF16),
        compiler_params=_params("parallel", "arbitrary"),
        name="expert_ffn1",
    )(blk_expert, xs, w1, w1, b1.reshape(n_exp, 1, two_f), b1.reshape(n_exp, 1, two_f))


def _ffn2_kernel(be_ref, a_ref, w_ref, b_ref, sw_ref, o_ref):
    del be_ref
    y = jnp.dot(a_ref[...], w_ref[0], preferred_element_type=F32) + b_ref[0]
    o_ref[...] = y * sw_ref[...]


def _ffn2(act, w2, b2, slot_w, blk_expert):
    n_slots, f = act.shape
    n_exp, _, d = w2.shape
    rows = MOE_ROWS
    tn = _tile(d, 1024)
    grid_spec = pltpu.PrefetchScalarGridSpec(
        num_scalar_prefetch=1,
        grid=(d // tn, n_slots // rows),
        in_specs=[pl.BlockSpec((rows, f), lambda j, i, be: (i, 0)),
                  pl.BlockSpec((1, f, tn), lambda j, i, be: (be[i], 0, j)),
                  pl.BlockSpec((1, 1, tn), lambda j, i, be: (be[i], 0, j)),
                  pl.BlockSpec((rows, 1), lambda j, i, be: (i, 0))],
        out_specs=pl.BlockSpec((rows, tn), lambda j, i, be: (i, j)),
    )
    return pl.pallas_call(
        _ffn2_kernel,
        grid_spec=grid_spec,
        out_shape=jax.ShapeDtypeStruct((n_slots, d), F32),
        compiler_params=_params("parallel", "arbitrary"),
        name="expert_ffn2",
    )(blk_expert, act, w2, b2.reshape(n_exp, 1, d), slot_w.reshape(n_slots, 1))


def _combine_kernel(dest_ref, x_ref, gt_ref, gf_ref, ys_hbm, o_ref, buf, sem, *, rows):
    def row_copy(i):
        return pltpu.make_async_copy(ys_hbm.at[pl.ds(dest_ref[i], 1)],
                                     buf.at[i % TOP_K, pl.ds(i // TOP_K, 1)], sem)

    def start(i, carry):
        row_copy(i).start()
        return carry

    def wait(i, carry):
        row_copy(i).wait()
        return carry

    lax.fori_loop(0, rows * TOP_K, start, 0)
    lax.fori_loop(0, rows * TOP_K, wait, 0)
    y = buf[0]
    for k in range(1, TOP_K):
        y = y + buf[k]
    x = x_ref[...] + gt_ref[0] * y
    o_ref[...] = x * lax.rsqrt(jnp.mean(x * x, axis=-1, keepdims=True) + EPS) * gf_ref[...]


def _combine(x1, gate, normf_g, ys, dest_flat, seq):
    m, d = x1.shape
    rows = _tile(seq, 128, SUBLANES)
    per_batch = seq // rows
    return pl.pallas_call(
        functools.partial(_combine_kernel, rows=rows),
        grid=(m // rows,),
        in_specs=[pl.BlockSpec((rows * TOP_K,), lambda i: (i,), memory_space=pltpu.SMEM),
                  pl.BlockSpec((rows, d), lambda i: (i, 0)),
                  pl.BlockSpec((1, 1, d), lambda i: (i // per_batch, 0, 0)),
                  pl.BlockSpec((1, d), lambda i: (0, 0)),
                  pl.BlockSpec(memory_space=pl.ANY)],
        out_specs=pl.BlockSpec((rows, d), lambda i: (i, 0)),
        out_shape=jax.ShapeDtypeStruct((m, d), F32),
        scratch_shapes=[pltpu.VMEM((TOP_K, rows, d), F32), pltpu.SemaphoreType.DMA(())],
        compiler_params=_params("arbitrary"),
        name="combine",
    )(dest_flat, x1, gate, normf_g.reshape(1, d), ys)


def kernel(x, c, ctx, c_ctx, w_mod, b_mod, norm1_g, w_in, conv_w, a_log, dt_bias, onorm_g, gm_ln_g, gm_ln_b,
           gm_ws, gm_bs, w_up_a, w_up_b, w_o, norm2_g, w_router, b_router, w1, b1, w2, b2, normf_g):
    depth = w_mod.shape[0]
    assert depth == 1, "single-layer block"
    bsz, seq, d = x.shape
    ctx_len = ctx.shape[1]
    n_heads = a_log.shape[-1]
    dn_w = n_heads * HEAD
    gm_w = gm_ln_g.shape[-1]
    n_exp = w_router.shape[-1]
    assert seq % GRID_W == 0 and seq % GM_CHUNK == 0 and seq % DN_CHUNK == 0 and ctx_len % DN_CHUNK == 0
    assert 4 * n_heads <= LANES and GM_CHUNK % (seq // GRID_W) == 0
    (w_mod, b_mod, norm1_g, w_in, conv_w, a_log, dt_bias, onorm_g, gm_ln_g, gm_ln_b, gm_ws, gm_bs, w_up_a, w_up_b,
     w_o, norm2_g, w_router, b_router, w1, b1, w2, b2) = (
        t[0] for t in (w_mod, b_mod, norm1_g, w_in, conv_w, a_log, dt_bias, onorm_g, gm_ln_g, gm_ln_b, gm_ws, gm_bs,
                       w_up_a, w_up_b, w_o, norm2_g, w_router, b_router, w1, b1, w2, b2))

    n_rows = -(-(bsz + 1) // SUBLANES) * SUBLANES
    cond = jnp.concatenate([c, c_ctx[None], jnp.zeros((n_rows - bsz - 1, d), F32)], axis=0)
    mod = _modulation(cond, w_mod, b_mod)
    sh1, sc1, gt1, sh2, sc2, gt2 = (mod[:, i * d:(i + 1) * d].reshape(n_rows, 1, d) for i in range(N_MOD))

    col_decay = 2 * dn_w
    col_q = col_decay + 4 * n_heads
    w_main = jnp.concatenate([w_in[:, :col_decay], w_in[:, col_q:]], axis=1).astype(BF16)
    w_dec = jnp.concatenate([w_in[:, col_decay:col_q], jnp.zeros((d, LANES - 4 * n_heads), F32)], axis=1).astype(BF16)
    col_z = 3 * dn_w
    col_gu = col_z + dn_w
    col_gv = col_gu + gm_w
    col_merge = col_gv + gm_w

    hc = _norm_modulate(ctx, norm1_g, sc1, sh1, lambda i: bsz).reshape(bsz * ctx_len, d)
    pc = _matmul(hc, w_main[:, :2 * dn_w], F32).reshape(bsz, ctx_len, 2 * dn_w)
    rawc = _matmul(hc, w_dec, F32).reshape(bsz, ctx_len, LANES)
    kvc = _short_conv(pc, conv_w, dn_w, 2)
    colsc, dmatc = _decay_prep(rawc, a_log, dt_bias, n_heads)
    zero_state = jnp.zeros((bsz, 2, n_heads, HEAD, HEAD), F32)
    _, _, ctx_state = _delta_scan(kvc, colsc, dmatc, zero_state, n_heads, False)

    h = _norm_modulate(x, norm1_g, sc1, sh1, lambda i: i).reshape(bsz * seq, d)
    p2d = _matmul(h, w_main, F32)
    p = p2d.reshape(bsz, seq, -1)
    raw = _matmul(h, w_dec, F32).reshape(bsz, seq, LANES)
    kvq = _short_conv(p, conv_w, dn_w, 3)
    cols, dmat = _decay_prep(raw, a_log, dt_bias, n_heads)
    o_f, o_b, _ = _delta_scan(kvq, cols, dmat, ctx_state, n_heads, True)
    y_a = _gated_norm(o_f, o_b, p, col_z, onorm_g).reshape(bsz * seq, dn_w)
    gvn = _gelu_layernorm(p, col_gv, gm_w, gm_ln_g, gm_ln_b)
    y_b = _spatial_gate(gvn, p, col_gu, gm_ws, gm_bs).reshape(bsz * seq, gm_w)
    merged = _merge(y_a, y_b, w_up_a.astype(BF16), w_up_b.astype(BF16), p2d, col_merge)
    x2d = x.reshape(bsz * seq, d)
    x1 = _out_proj_residual(merged, w_o.astype(BF16), x2d, gt1, seq)

    h2, ti, tw, rk, cnt = _router(x1, norm2_g, sc2, sh2, w_router, b_router, seq)
    n_tok = bsz * seq
    top_i = ti[:, :TOP_K]
    counts = cnt[0].astype(I32)
    padded = (counts + MOE_ROWS - 1) // MOE_ROWS * MOE_ROWS
    pad_end = jnp.cumsum(padded)
    pad_start = pad_end - padded
    dest = (pad_start[top_i] + rk[:, :TOP_K]).reshape(-1)
    n_slots = n_tok * TOP_K + n_exp * MOE_ROWS
    tok_flat = jnp.repeat(jnp.arange(n_tok, dtype=I32), TOP_K)
    slot_tok = jnp.zeros((n_slots,), I32).at[dest].set(tok_flat)
    slot_w = jnp.zeros((n_slots,), F32).at[dest].set(tw[:, :TOP_K].reshape(-1))
    blk_start = jnp.arange(n_slots // MOE_ROWS, dtype=I32) * MOE_ROWS
    blk_expert = jnp.minimum(jnp.searchsorted(pad_end, blk_start, side="right"), n_exp - 1).astype(I32)
    xs = _dispatch(h2, slot_tok)
    act = _ffn1(xs, w1.astype(BF16), b1, blk_expert)
    ys = _ffn2(act, w2.astype(BF16), b2, slot_w, blk_expert)
    out = _combine(x1, gt2, normf_g, ys, dest, seq)
    return out.reshape(bsz, seq, d)
```

```python
import functools

import jax
import jax.numpy as jnp
from jax import lax
from jax.experimental import pallas as pl
from jax.experimental.pallas import tpu as pltpu

F32 = jnp.float32
BF16 = jnp.bfloat16
I32 = jnp.int32
HIGHEST = lax.Precision.HIGHEST

EPS = 1e-6
N_MOD = 6
GRID_W = 64
HEAD = 128
DN_CHUNK = 64
GM_CHUNK = 128
TOP_K = 4
SWIGLU_LIMIT = 7.0
SWIGLU_ALPHA = 1.702
MOE_ROWS = 256
LANES = 128
SUBLANES = 8
VMEM_LIMIT = 56 * 1024 * 1024

NT_DIMS = (((1,), (1,)), ((), ()))
TN_DIMS = (((0,), (0,)), ((), ()))


def _tile(dim, pref, align=LANES):
    t = min(pref, dim) // align * align
    while t >= align:
        if dim % t == 0:
            return t
        t -= align
    return dim


def _params(*sem):
    return pltpu.CompilerParams(dimension_semantics=sem, vmem_limit_bytes=VMEM_LIMIT)


def _silu(x):
    return x * jax.nn.sigmoid(x)


def _gelu_tanh(x):
    c = 0.7978845608028654
    return 0.5 * x * (1.0 + jnp.tanh(c * (x + 0.044715 * (x * x * x))))


def _mod_kernel(c_ref, w_ref, b_ref, o_ref):
    s = _silu(c_ref[...]).astype(BF16)
    o_ref[...] = jnp.dot(s, w_ref[...].astype(BF16), preferred_element_type=F32) + b_ref[...]


def _modulation(cond, w_mod, b_mod):
    rows, d = cond.shape
    n = w_mod.shape[1]
    tn = _tile(n, 512)
    return pl.pallas_call(
        _mod_kernel,
        grid=(n // tn,),
        in_specs=[pl.BlockSpec((rows, d), lambda j: (0, 0)),
                  pl.BlockSpec((d, tn), lambda j: (0, j)),
                  pl.BlockSpec((1, tn), lambda j: (0, j))],
        out_specs=pl.BlockSpec((rows, tn), lambda j: (0, j)),
        out_shape=jax.ShapeDtypeStruct((rows, n), F32),
        compiler_params=_params("parallel"),
        name="modulation",
    )(cond, w_mod, b_mod.reshape(1, n))


def _rms_mod(x, g, sc, sh):
    y = x * lax.rsqrt(jnp.mean(x * x, axis=-1, keepdims=True) + EPS)
    return (y * g) * (1.0 + sc) + sh


def _normmod_kernel(x_ref, g_ref, sc_ref, sh_ref, o_ref):
    o_ref[0] = _rms_mod(x_ref[0], g_ref[...], sc_ref[0], sh_ref[0]).astype(o_ref.dtype)


def _norm_modulate(x, g, sc, sh, row_of_batch):
    b, l, d = x.shape
    tl = _tile(l, 512, SUBLANES)
    return pl.pallas_call(
        _normmod_kernel,
        grid=(b, l // tl),
        in_specs=[pl.BlockSpec((1, tl, d), lambda i, j: (i, j, 0)),
                  pl.BlockSpec((1, d), lambda i, j: (0, 0)),
                  pl.BlockSpec((1, 1, d), lambda i, j: (row_of_batch(i), 0, 0)),
                  pl.BlockSpec((1, 1, d), lambda i, j: (row_of_batch(i), 0, 0))],
        out_specs=pl.BlockSpec((1, tl, d), lambda i, j: (i, j, 0)),
        out_shape=jax.ShapeDtypeStruct((b, l, d), BF16),
        compiler_params=_params("parallel", "parallel"),
        name="norm_modulate",
    )(x, g.reshape(1, d), sc, sh)


def _mm_kernel(a_ref, w_ref, o_ref):
    o_ref[...] = jnp.dot(a_ref[...], w_ref[...], preferred_element_type=F32).astype(o_ref.dtype)


def _matmul(a, w, out_dtype, tm_pref=1024, tn_pref=1024):
    m, k = a.shape
    n = w.shape[1]
    tm = _tile(m, tm_pref, SUBLANES)
    tn = _tile(n, tn_pref)
    return pl.pallas_call(
        _mm_kernel,
        grid=(m // tm, n // tn),
        in_specs=[pl.BlockSpec((tm, k), lambda i, j: (i, 0)),
                  pl.BlockSpec((k, tn), lambda i, j: (0, j))],
        out_specs=pl.BlockSpec((tm, tn), lambda i, j: (i, j)),
        out_shape=jax.ShapeDtypeStruct((m, n), out_dtype),
        compiler_params=_params("parallel", "parallel"),
        name="matmul",
    )(a, w)


def _conv_kernel(prev_ref, cur_ref, next_ref, w_ref, o_ref, *, tl, n_t, tc, k_tiles, conv_k):
    i = pl.program_id(1)
    j = pl.program_id(2)
    prev = jnp.where(i > 0, prev_ref[0], 0.0)
    nxt = jnp.where(i < n_t - 1, next_ref[0], 0.0)
    xx = jnp.concatenate([prev, cur_ref[0], nxt], axis=0)
    n = tl + 2 * SUBLANES
    acc = None
    for jj in range(conv_k):
        s = jj - conv_k // 2
        shifted = xx if s == 0 else pltpu.roll(xx, (-s) % n, 0)
        term = shifted[SUBLANES:SUBLANES + tl] * w_ref[jj:jj + 1, :]
        acc = term if acc is None else acc + term
    y = _silu(acc)
    is_k = j < k_tiles
    is_q = j >= 2 * k_tiles
    outs = []
    for hh in range(tc // HEAD):
        yh = y[:, hh * HEAD:(hh + 1) * HEAD]
        inv = lax.rsqrt(jnp.sum(yh * yh, axis=-1, keepdims=True) + EPS)
        scale = jnp.where(is_k, inv, jnp.where(is_q, inv * (HEAD ** -0.5), 1.0))
        outs.append(yh * scale)
    o_ref[0] = jnp.concatenate(outs, axis=-1)


def _short_conv(p, conv_w, width, n_sections):
    b, l, _ = p.shape
    conv_k = conv_w.shape[0]
    c = n_sections * width
    tc = _tile(width, 512)
    tl = _tile(l, 512, SUBLANES)
    n_t = l // tl
    sub = tl // SUBLANES
    kern = functools.partial(_conv_kernel, tl=tl, n_t=n_t, tc=tc, k_tiles=width // tc, conv_k=conv_k)
    return pl.pallas_call(
        kern,
        grid=(b, n_t, c // tc),
        in_specs=[
            pl.BlockSpec((1, SUBLANES, tc), lambda bi, i, j: (bi, jnp.maximum(i * sub - 1, 0), j)),
            pl.BlockSpec((1, tl, tc), lambda bi, i, j: (bi, i, j)),
            pl.BlockSpec((1, SUBLANES, tc), lambda bi, i, j: (bi, jnp.minimum((i + 1) * sub, l // SUBLANES - 1), j)),
            pl.BlockSpec((conv_k, tc), lambda bi, i, j: (0, j)),
        ],
        out_specs=pl.BlockSpec((1, tl, tc), lambda bi, i, j: (bi, i, j)),
        out_shape=jax.ShapeDtypeStruct((b, l, c), F32),
        compiler_params=_params("parallel", "parallel", "parallel"),
        name="short_conv",
    )(p, p, p, conv_w[:, :c])


def _decay_kernel(raw_ref, alog_ref, dtb_ref, cols_ref, dmat_ref, *, n_heads):
    h = n_heads
    c = DN_CHUNK
    raw = raw_ref[0]
    x = raw + dtb_ref[...]
    softplus = jnp.maximum(x, 0.0) + jnp.log1p(jnp.exp(-jnp.abs(x)))
    gs = -jnp.exp(alog_ref[...]) * softplus
    beta = jax.nn.sigmoid(raw)
    ii = lax.broadcasted_iota(I32, (c, c), 0)
    jj = lax.broadcasted_iota(I32, (c, c), 1)
    tri = [jj <= ii, jj >= ii]
    trif = [t.astype(F32) for t in tri]
    eye = (lax.broadcasted_iota(I32, (LANES, LANES), 0) == lax.broadcasted_iota(I32, (LANES, LANES), 1)).astype(F32)
    gs_t = lax.dot_general(eye, gs, NT_DIMS, precision=HIGHEST, preferred_element_type=F32)
    lane = lax.broadcasted_iota(I32, (c, LANES), 1)
    g_dir = [jnp.dot(trif[d], gs, precision=HIGHEST, preferred_element_type=F32) for d in range(2)]
    g_cum = jnp.where(lane < h, g_dir[0], g_dir[1])
    g_last = jnp.where(lane < h, g_dir[0][c - 1:c, :], g_dir[1][0:1, :])
    cols_ref[0] = jnp.concatenate([beta, jnp.exp(g_cum), jnp.exp(g_last - g_cum), jnp.exp(g_last)], axis=-1)
    for d in range(2):
        g_row = lax.dot_general(gs_t, trif[d], NT_DIMS, precision=HIGHEST, preferred_element_type=F32)
        mats = []
        for hh in range(h):
            col = d * h + hh
            diff = g_dir[d][:, col:col + 1] - g_row[col:col + 1, :]
            mats.append(jnp.where(tri[d], jnp.exp(jnp.where(tri[d], diff, 0.0)), 0.0))
        dmat_ref[0, d] = jnp.concatenate(mats, axis=-1)


def _decay_prep(raw, a_log, dt_bias, n_heads):
    b, l, _ = raw.shape
    h = n_heads
    pad = LANES - 2 * h
    alog = jnp.concatenate([a_log.reshape(1, 2 * h), jnp.zeros((1, pad), F32)], axis=-1)
    dtb = jnp.concatenate([dt_bias.reshape(1, 2 * h), jnp.zeros((1, pad), F32)], axis=-1)
    n = l // DN_CHUNK
    return pl.pallas_call(
        functools.partial(_decay_kernel, n_heads=h),
        grid=(b, n),
        in_specs=[pl.BlockSpec((1, DN_CHUNK, LANES), lambda i, j: (i, j, 0)),
                  pl.BlockSpec((1, LANES), lambda i, j: (0, 0)),
                  pl.BlockSpec((1, LANES), lambda i, j: (0, 0))],
        out_specs=[pl.BlockSpec((1, DN_CHUNK, 4 * LANES), lambda i, j: (i, j, 0)),
                   pl.BlockSpec((1, 2, DN_CHUNK, h * DN_CHUNK), lambda i, j: (i, 0, j, 0))],
        out_shape=[jax.ShapeDtypeStruct((b, l, 4 * LANES), F32),
                   jax.ShapeDtypeStruct((b, 2, l, h * DN_CHUNK), F32)],
        compiler_params=_params("parallel", "parallel"),
        name="decay_prep",
    )(raw, alog, dtb)


def _dot16(a, b, dims=None):
    a = a.astype(BF16)
    b = b.astype(BF16)
    if dims is None:
        return jnp.dot(a, b, preferred_element_type=F32)
    return lax.dot_general(a, b, dims, preferred_element_type=F32)


def _split16(a):
    hi = a.astype(BF16)
    return hi, (a - hi.astype(F32)).astype(BF16)


def _dot_split(a, b):
    ah, al = _split16(a)
    bh, bl = _split16(b)
    return (jnp.dot(ah, bh, preferred_element_type=F32) + jnp.dot(ah, bl, preferred_element_type=F32)
            + jnp.dot(al, bh, preferred_element_type=F32))


def _unit_triangular_inverses(mats, ii, jj):
    eye = (ii == jj).astype(F32)
    blk8 = (ii // 8) == (jj // 8)
    a0 = [jnp.where(blk8, a, 0.0) for a in mats]
    a2 = [_dot_split(x, x) for x in a0]
    a4 = [_dot_split(x, x) for x in a2]
    p = [eye - x for x in a0]
    p = [x + _dot_split(x, y) for x, y in zip(p, a2)]
    p = [x + _dot_split(x, y) for x, y in zip(p, a4)]
    for s in (8, 16, 32):
        sel = ((ii // (2 * s)) == (jj // (2 * s))) & ((ii // s) != (jj // s))
        t = [_dot_split(jnp.where(sel, a, 0.0), x) for a, x in zip(mats, p)]
        p = [x - _dot_split(x, y) for x, y in zip(p, t)]
    return p


def _delta_kernel(*refs, hb, n_heads, with_q):
    if with_q:
        (kf_ref, vf_ref, qf_ref, kb_ref, vb_ref, qb_ref, df_ref, db_ref, cf_ref, cb_ref, s0_ref,
         of_ref, ob_ref, sfin_ref, s_ref) = refs
        q_refs = (qf_ref, qb_ref)
        o_refs = (of_ref, ob_ref)
    else:
        (kf_ref, vf_ref, kb_ref, vb_ref, df_ref, db_ref, cf_ref, cb_ref, s0_ref, sfin_ref, s_ref) = refs
    k_refs = (kf_ref, kb_ref)
    v_refs = (vf_ref, vb_ref)
    d_refs = (df_ref, db_ref)
    c_refs = (cf_ref, cb_ref)
    hg = pl.program_id(1)
    n = pl.program_id(2)
    c = DN_CHUNK

    @pl.when(n == 0)
    def _():
        s_ref[...] = s0_ref[0]

    ii = lax.broadcasted_iota(I32, (c, c), 0)
    jj = lax.broadcasted_iota(I32, (c, c), 1)
    lane = lax.broadcasted_iota(I32, (c, LANES), 1)

    def column(x, idx):
        return jnp.sum(jnp.where(lane == idx, x, 0.0), axis=-1, keepdims=True)

    chains = [(d, hh) for d in range(2) for hh in range(hb)]
    cols = [c_refs[d][0] for d in range(2)]
    beta, eg, ekend, eglast, k, v, dm = [], [], [], [], [], [], []
    for d, hh in chains:
        head = hg * hb + hh
        beta.append(column(cols[d][:, 0:LANES], (2 + d) * n_heads + head))
        eg.append(column(cols[d][:, LANES:2 * LANES], d * n_heads + head))
        ekend.append(column(cols[d][:, 2 * LANES:3 * LANES], d * n_heads + head))
        eglast.append(column(cols[d][:, 3 * LANES:4 * LANES], d * n_heads + head)[0:1, :])
        k.append(k_refs[d][0][:, hh * HEAD:(hh + 1) * HEAD])
        v.append(v_refs[d][0][:, hh * HEAD:(hh + 1) * HEAD])
        dm.append(d_refs[d][0, 0][:, hh * c:(hh + 1) * c])
    strict = [(jj < ii) if d == 0 else (jj > ii) for d, _ in chains]
    kb = [x * y for x, y in zip(k, beta)]
    kk = [_dot16(x, y, NT_DIMS) for x, y in zip(kb, k)]
    a = [jnp.where(m, x * y, 0.0) for m, x, y in zip(strict, kk, dm)]
    t = _unit_triangular_inverses(a, ii, jj)
    u = [_dot16(x, y * z) for x, y, z in zip(t, v, beta)]
    w = [_dot16(x, y * z) for x, y, z in zip(t, kb, eg)]
    s = [s_ref[d, hh] for d, hh in chains]
    v_new = [x - _dot16(y, z) for x, y, z in zip(u, w, s)]
    if with_q:
        q = [q_refs[d][0][:, hh * HEAD:(hh + 1) * HEAD] for d, hh in chains]
        qk = [_dot16(x, y, NT_DIMS) * z for x, y, z in zip(q, k, dm)]
        o = [_dot16(x * y, z) + _dot16(r, vn) for x, y, z, r, vn in zip(q, eg, s, qk, v_new)]
        for d in range(2):
            o_refs[d][0] = jnp.concatenate([x for x, (dd, _) in zip(o, chains) if dd == d], axis=-1)
    upd = [_dot16(x * y, vn, TN_DIMS) for x, y, vn in zip(k, ekend, v_new)]
    for (d, hh), x, y, z in zip(chains, s, eglast, upd):
        s_ref[d, hh] = x * y + z

    @pl.when(n == pl.num_programs(2) - 1)
    def _():
        sfin_ref[0] = s_ref[...]


def _delta_scan(kvq, cols, dmat, s0, n_heads, with_q):
    b, l, _ = kvq.shape
    h = n_heads
    hb = 4 if h % 4 == 0 else (2 if h % 2 == 0 else 1)
    c = DN_CHUNK
    n = l // c
    wblk = h // hb

    def sec(section, rev):
        def imap(bi, g, j):
            return (bi, (n - 1 - j) if rev else j, section * wblk + g)
        return pl.BlockSpec((1, c, hb * HEAD), imap)

    def dspec(d):
        return pl.BlockSpec((1, 1, c, hb * c), lambda bi, g, j: (bi, d, (n - 1 - j) if d else j, g))

    def cspec(d):
        return pl.BlockSpec((1, c, 4 * LANES), lambda bi, g, j: (bi, (n - 1 - j) if d else j, 0))

    state_spec = pl.BlockSpec((1, 2, hb, HEAD, HEAD), lambda bi, g, j: (bi, 0, g, 0, 0))
    n_sec = 3 if with_q else 2
    in_specs = [sec(s, False) for s in range(n_sec)] + [sec(s, True) for s in range(n_sec)]
    in_specs += [dspec(0), dspec(1), cspec(0), cspec(1), state_spec]
    args = [kvq] * (2 * n_sec) + [dmat, dmat, cols, cols, s0]
    out_specs = [state_spec]
    out_shape = [jax.ShapeDtypeStruct((b, 2, h, HEAD, HEAD), F32)]
    if with_q:
        ospec = [pl.BlockSpec((1, c, hb * HEAD), lambda bi, g, j: (bi, j, g)),
                 pl.BlockSpec((1, c, hb * HEAD), lambda bi, g, j: (bi, n - 1 - j, g))]
        out_specs = ospec + out_specs
        out_shape = [jax.ShapeDtypeStruct((b, l, h * HEAD), F32)] * 2 + out_shape
    res = pl.pallas_call(
        functools.partial(_delta_kernel, hb=hb, n_heads=h, with_q=with_q),
        grid=(b, wblk, n),
        in_specs=in_specs,
        out_specs=out_specs,
        out_shape=out_shape,
        scratch_shapes=[pltpu.VMEM((2, hb, HEAD, HEAD), F32)],
        compiler_params=_params("parallel", "parallel", "arbitrary"),
        name="delta_scan_q" if with_q else "delta_scan_state",
    )(*args)
    if with_q:
        return res[0], res[1], res[2]
    return None, None, res[0]


def _gated_norm_kernel(of_ref, ob_ref, z_ref, g_ref, o_ref):
    o = of_ref[0] + ob_ref[0]
    z = z_ref[0]
    outs = []
    for hh in range(o.shape[-1] // HEAD):
        oh = o[:, hh * HEAD:(hh + 1) * HEAD]
        y = oh * lax.rsqrt(jnp.mean(oh * oh, axis=-1, keepdims=True) + EPS) * g_ref[...]
        outs.append(y * _silu(z[:, hh * HEAD:(hh + 1) * HEAD]))
    o_ref[0] = jnp.concatenate(outs, axis=-1).astype(o_ref.dtype)


def _gated_norm(o_f, o_b, p, z_col, onorm_g):
    b, l, w = o_f.shape
    tl = _tile(l, 512, SUBLANES)
    tc = _tile(w, 512)
    zb = z_col // tc
    return pl.pallas_call(
        _gated_norm_kernel,
        grid=(b, l // tl, w // tc),
        in_specs=[pl.BlockSpec((1, tl, tc), lambda i, j, k: (i, j, k)),
                  pl.BlockSpec((1, tl, tc), lambda i, j, k: (i, j, k)),
                  pl.BlockSpec((1, tl, tc), lambda i, j, k: (i, j, zb + k)),
                  pl.BlockSpec((1, HEAD), lambda i, j, k: (0, 0))],
        out_specs=pl.BlockSpec((1, tl, tc), lambda i, j, k: (i, j, k)),
        out_shape=jax.ShapeDtypeStruct((b, l, w), BF16),
        compiler_params=_params("parallel", "parallel", "parallel"),
        name="gated_norm",
    )(o_f, o_b, p, onorm_g.reshape(1, HEAD))


def _gelu_ln_kernel(x_ref, g_ref, b_ref, o_ref):
    x = _gelu_tanh(x_ref[0])
    mu = jnp.mean(x, axis=-1, keepdims=True)
    xc = x - mu
    var = jnp.mean(xc * xc, axis=-1, keepdims=True)
    o_ref[0] = (xc * lax.rsqrt(var + EPS)) * g_ref[...] + b_ref[...]


def _gelu_layernorm(p, col, width, g, bias):
    b, l, _ = p.shape
    tl = _tile(l, 256, SUBLANES)
    cb = col // width
    return pl.pallas_call(
        _gelu_ln_kernel,
        grid=(b, l // tl),
        in_specs=[pl.BlockSpec((1, tl, width), lambda i, j: (i, j, cb)),
                  pl.BlockSpec((1, width), lambda i, j: (0, 0)),
                  pl.BlockSpec((1, width), lambda i, j: (0, 0))],
        out_specs=pl.BlockSpec((1, tl, width), lambda i, j: (i, j, 0)),
        out_shape=jax.ShapeDtypeStruct((b, l, width), F32),
        compiler_params=_params("parallel", "parallel"),
        name="gelu_layernorm",
    )(p, g.reshape(1, width), bias.reshape(1, width))


def _spatial_kernel(gv_ref, gu_ref, ws_ref, bs_ref, o_ref, *, seq, row_groups, grid_w):
    g = pl.program_id(1)
    w = ws_ref[0].astype(BF16)
    bias = bs_ref[0]
    n_chunks = seq // GM_CHUNK
    rows = seq // grid_w
    cols_per_chunk = GM_CHUNK // rows

    @pl.when(g < row_groups)
    def _():
        for n in range(n_chunks):
            sl = pl.ds(n * GM_CHUNK, GM_CHUNK)
            s = jnp.dot(w, gv_ref[0, sl, :].astype(BF16), preferred_element_type=F32) + bias
            o_ref[0, sl, :] = (_gelu_tanh(gu_ref[0, sl, :]) * s).astype(o_ref.dtype)

    @pl.when(g >= row_groups)
    def _():
        for n in range(n_chunks):
            sls = [pl.ds(n * cols_per_chunk + cc, rows, stride=grid_w) for cc in range(cols_per_chunk)]
            v = jnp.concatenate([gv_ref[0, sl, :] for sl in sls], axis=0)
            s = jnp.dot(w, v.astype(BF16), preferred_element_type=F32) + bias
            for cc, sl in enumerate(sls):
                o_ref[0, sl, :] = (_gelu_tanh(gu_ref[0, sl, :]) * s[cc * rows:(cc + 1) * rows]).astype(o_ref.dtype)


def _spatial_gate(gvn, p, gu_col, gm_ws, gm_bs):
    b, l, width = gvn.shape
    groups = gm_ws.shape[0]
    gub = gu_col // HEAD
    kern = functools.partial(_spatial_kernel, seq=l, row_groups=groups // 2, grid_w=GRID_W)
    return pl.pallas_call(
        kern,
        grid=(b, groups),
        in_specs=[pl.BlockSpec((1, l, HEAD), lambda i, g: (i, 0, g)),
                  pl.BlockSpec((1, l, HEAD), lambda i, g: (i, 0, gub + g)),
                  pl.BlockSpec((1, GM_CHUNK, GM_CHUNK), lambda i, g: (g, 0, 0)),
                  pl.BlockSpec((1, GM_CHUNK, 1), lambda i, g: (g, 0, 0))],
        out_specs=pl.BlockSpec((1, l, HEAD), lambda i, g: (i, 0, g)),
        out_shape=jax.ShapeDtypeStruct((b, l, width), F32),
        compiler_params=_params("parallel", "parallel"),
        name="spatial_gate",
    )(gvn, p, gm_ws, gm_bs.reshape(groups, GM_CHUNK, 1))


def _merge_kernel(ya_ref, yb_ref, wa_ref, wb_ref, ga_ref, gb_ref, o_ref):
    pa = jnp.dot(ya_ref[...], wa_ref[...], preferred_element_type=F32)
    pb = jnp.dot(yb_ref[...].astype(BF16), wb_ref[...], preferred_element_type=F32)
    o_ref[...] = (jax.nn.sigmoid(ga_ref[...]) * pa + jax.nn.sigmoid(gb_ref[...]) * pb).astype(o_ref.dtype)


def _merge(ya, yb, wa, wb, p2d, merge_col):
    m, ka = ya.shape
    kb = yb.shape[1]
    d = wa.shape[1]
    tm = _tile(m, 512, SUBLANES)
    tn = _tile(d, 1024)
    ca = merge_col // tn
    cb = (merge_col + d) // tn
    return pl.pallas_call(
        _merge_kernel,
        grid=(m // tm, d // tn),
        in_specs=[pl.BlockSpec((tm, ka), lambda i, j: (i, 0)),
                  pl.BlockSpec((tm, kb), lambda i, j: (i, 0)),
                  pl.BlockSpec((ka, tn), lambda i, j: (0, j)),
                  pl.BlockSpec((kb, tn), lambda i, j: (0, j)),
                  pl.BlockSpec((tm, tn), lambda i, j: (i, ca + j)),
                  pl.BlockSpec((tm, tn), lambda i, j: (i, cb + j))],
        out_specs=pl.BlockSpec((tm, tn), lambda i, j: (i, j)),
        out_shape=jax.ShapeDtypeStruct((m, d), BF16),
        compiler_params=_params("parallel", "parallel"),
        name="merge",
    )(ya, yb, wa, wb, p2d, p2d)


def _oproj_kernel(a_ref, w_ref, x_ref, gt_ref, o_ref):
    y = jnp.dot(a_ref[...], w_ref[...], preferred_element_type=F32)
    o_ref[...] = x_ref[...] + gt_ref[0] * y


def _out_proj_residual(a, w, x2d, gate, seq):
    m, k = a.shape
    d = w.shape[1]
    tm = _tile(seq, 1024, SUBLANES)
    tn = _tile(d, 1024)
    per_batch = seq // tm
    return pl.pallas_call(
        _oproj_kernel,
        grid=(m // tm, d // tn),
        in_specs=[pl.BlockSpec((tm, k), lambda i, j: (i, 0)),
                  pl.BlockSpec((k, tn), lambda i, j: (0, j)),
                  pl.BlockSpec((tm, tn), lambda i, j: (i, j)),
                  pl.BlockSpec((1, 1, tn), lambda i, j: (i // per_batch, 0, j))],
        out_specs=pl.BlockSpec((tm, tn), lambda i, j: (i, j)),
        out_shape=jax.ShapeDtypeStruct((m, d), F32),
        compiler_params=_params("parallel", "parallel"),
        name="out_proj",
    )(a, w, x2d, gate)


def _router_kernel(x_ref, g_ref, sc_ref, sh_ref, wr_ref, br_ref, h_ref, ti_ref, tw_ref, rk_ref, cnt_ref, carry,
                   *, n_exp, tm):
    i = pl.program_id(0)

    @pl.when(i == 0)
    def _():
        carry[...] = jnp.zeros_like(carry)

    h2 = _rms_mod(x_ref[...], g_ref[...], sc_ref[0], sh_ref[0])
    h_ref[...] = h2
    logits = jnp.dot(h2, wr_ref[...], precision=HIGHEST, preferred_element_type=F32) + br_ref[...]
    lane = lax.broadcasted_iota(I32, (tm, n_exp), 1)
    work = logits
    member = jnp.zeros((tm, n_exp), F32)
    vals, ids = [], []
    for _ in range(TOP_K):
        mx = jnp.max(work, axis=-1, keepdims=True)
        idx = jnp.min(jnp.where(work == mx, lane, n_exp), axis=-1, keepdims=True)
        hit = lane == idx
        vals.append(mx)
        ids.append(idx)
        work = jnp.where(hit, -jnp.inf, work)
        member = member + hit.astype(F32)
    exps = [jnp.exp(v - vals[0]) for v in vals]
    den = exps[0]
    for e in exps[1:]:
        den = den + e
    rr = lax.broadcasted_iota(I32, (tm, tm), 0)
    cc = lax.broadcasted_iota(I32, (tm, tm), 1)
    below = (cc < rr).astype(BF16)
    rank_all = jnp.dot(below, member.astype(BF16), preferred_element_type=F32) + carry[...]
    out_lane = lax.broadcasted_iota(I32, (tm, LANES), 1)
    ti = jnp.zeros((tm, LANES), I32)
    tw = jnp.zeros((tm, LANES), F32)
    rk = jnp.zeros((tm, LANES), F32)
    for k in range(TOP_K):
        rank_k = jnp.sum(jnp.where(lane == ids[k], rank_all, 0.0), axis=-1, keepdims=True)
        ti = jnp.where(out_lane == k, ids[k], ti)
        tw = jnp.where(out_lane == k, exps[k] / den, tw)
        rk = jnp.where(out_lane == k, rank_k, rk)
    ti_ref[...] = ti
    tw_ref[...] = tw
    rk_ref[...] = rk.astype(I32)
    carry[...] = carry[...] + jnp.sum(member, axis=0, keepdims=True)
    cnt_ref[...] = carry[...]


def _router(x2d, g, sc, sh, w_router, b_router, seq):
    m, d = x2d.shape
    n_exp = w_router.shape[1]
    tm = _tile(seq, 256, SUBLANES)
    per_batch = seq // tm
    kern = functools.partial(_router_kernel, n_exp=n_exp, tm=tm)
    row = lambda i: (i, 0)
    return pl.pallas_call(
        kern,
        grid=(m // tm,),
        in_specs=[pl.BlockSpec((tm, d), row),
                  pl.BlockSpec((1, d), lambda i: (0, 0)),
                  pl.BlockSpec((1, 1, d), lambda i: (i // per_batch, 0, 0)),
                  pl.BlockSpec((1, 1, d), lambda i: (i // per_batch, 0, 0)),
                  pl.BlockSpec((d, n_exp), lambda i: (0, 0)),
                  pl.BlockSpec((1, n_exp), lambda i: (0, 0))],
        out_specs=[pl.BlockSpec((tm, d), row),
                   pl.BlockSpec((tm, LANES), row),
                   pl.BlockSpec((tm, LANES), row),
                   pl.BlockSpec((tm, LANES), row),
                   pl.BlockSpec((1, n_exp), lambda i: (0, 0))],
        out_shape=[jax.ShapeDtypeStruct((m, d), F32),
                   jax.ShapeDtypeStruct((m, LANES), I32),
                   jax.ShapeDtypeStruct((m, LANES), F32),
                   jax.ShapeDtypeStruct((m, LANES), I32),
                   jax.ShapeDtypeStruct((1, n_exp), F32)],
        scratch_shapes=[pltpu.VMEM((1, n_exp), F32)],
        compiler_params=_params("arbitrary"),
        name="router",
    )(x2d, g.reshape(1, d), sc, sh, w_router, b_router.reshape(1, n_exp))


def _dispatch_kernel(tok_ref, h_hbm, xs_ref, buf, sem, *, rows):
    def row_copy(i):
        return pltpu.make_async_copy(h_hbm.at[pl.ds(tok_ref[i], 1)], buf.at[pl.ds(i, 1)], sem)

    def start(i, carry):
        row_copy(i).start()
        return carry

    def wait(i, carry):
        row_copy(i).wait()
        return carry

    lax.fori_loop(0, rows, start, 0)
    lax.fori_loop(0, rows, wait, 0)
    xs_ref[...] = buf[...].astype(xs_ref.dtype)


def _dispatch(h2, slot_tok):
    n_slots = slot_tok.shape[0]
    d = h2.shape[1]
    rows = MOE_ROWS
    return pl.pallas_call(
        functools.partial(_dispatch_kernel, rows=rows),
        grid=(n_slots // rows,),
        in_specs=[pl.BlockSpec((rows,), lambda i: (i,), memory_space=pltpu.SMEM),
                  pl.BlockSpec(memory_space=pl.ANY)],
        out_specs=pl.BlockSpec((rows, d), lambda i: (i, 0)),
        out_shape=jax.ShapeDtypeStruct((n_slots, d), BF16),
        scratch_shapes=[pltpu.VMEM((rows, d), h2.dtype), pltpu.SemaphoreType.DMA(())],
        compiler_params=_params("arbitrary"),
        name="dispatch",
    )(slot_tok, h2)


def _ffn1_kernel(be_ref, new_ref, xs_ref, wg_ref, wl_ref, bg_ref, bl_ref, o_ref, wg16, wl16):
    del be_ref

    @pl.when(new_ref[pl.program_id(1)] == 1)
    def _():
        wg16[...] = wg_ref[0].astype(BF16)
        wl16[...] = wl_ref[0].astype(BF16)

    x = xs_ref[...]
    glu = jnp.dot(x, wg16[...], preferred_element_type=F32) + bg_ref[0]
    lin = jnp.dot(x, wl16[...], preferred_element_type=F32) + bl_ref[0]
    glu = jnp.minimum(glu, SWIGLU_LIMIT)
    lin = jnp.clip(lin, -SWIGLU_LIMIT, SWIGLU_LIMIT)
    o_ref[...] = (glu * jax.nn.sigmoid(SWIGLU_ALPHA * glu) * (lin + 1.0)).astype(o_ref.dtype)


def _ffn1(xs, w1, b1, blk_expert, blk_new):
    n_slots, d = xs.shape
    n_exp, _, two_f = w1.shape
    f = two_f // 2
    rows = MOE_ROWS
    tn = _tile(f, 512)
    nj = f // tn
    grid_spec = pltpu.PrefetchScalarGridSpec(
        num_scalar_prefetch=2,
        grid=(nj, n_slots // rows),
        in_specs=[pl.BlockSpec((rows, d), lambda j, i, be, nw: (i, 0)),
                  pl.BlockSpec((1, d, tn), lambda j, i, be, nw: (be[i], 0, j)),
                  pl.BlockSpec((1, d, tn), lambda j, i, be, nw: (be[i], 0, nj + j)),
                  pl.BlockSpec((1, 1, tn), lambda j, i, be, nw: (be[i], 0, j)),
                  pl.BlockSpec((1, 1, tn), lambda j, i, be, nw: (be[i], 0, nj + j))],
        out_specs=pl.BlockSpec((rows, tn), lambda j, i, be, nw: (i, j)),
        scratch_shapes=[pltpu.VMEM((d, tn), BF16), pltpu.VMEM((d, tn), BF16)],
    )
    return pl.pallas_call(
        _ffn1_kernel,
        grid_spec=grid_spec,
        out_shape=jax.ShapeDtypeStruct((n_slots, f), BF16),
        compiler_params=_params("arbitrary", "arbitrary"),
        name="expert_ffn1",
    )(blk_expert, blk_new, xs, w1, w1, b1.reshape(n_exp, 1, two_f), b1.reshape(n_exp, 1, two_f))


def _ffn2_kernel(be_ref, new_ref, a_ref, w_ref, b_ref, o_ref, w16):
    del be_ref

    @pl.when(new_ref[pl.program_id(1)] == 1)
    def _():
        w16[...] = w_ref[0].astype(BF16)

    o_ref[...] = jnp.dot(a_ref[...], w16[...], preferred_element_type=F32) + b_ref[0]


def _ffn2(act, w2, b2, blk_expert, blk_new):
    n_slots, f = act.shape
    n_exp, _, d = w2.shape
    rows = MOE_ROWS
    tn = _tile(d, 1024)
    grid_spec = pltpu.PrefetchScalarGridSpec(
        num_scalar_prefetch=2,
        grid=(d // tn, n_slots // rows),
        in_specs=[pl.BlockSpec((rows, f), lambda j, i, be, nw: (i, 0)),
                  pl.BlockSpec((1, f, tn), lambda j, i, be, nw: (be[i], 0, j)),
                  pl.BlockSpec((1, 1, tn), lambda j, i, be, nw: (be[i], 0, j))],
        out_specs=pl.BlockSpec((rows, tn), lambda j, i, be, nw: (i, j)),
        scratch_shapes=[pltpu.VMEM((f, tn), BF16)],
    )
    return pl.pallas_call(
        _ffn2_kernel,
        grid_spec=grid_spec,
        out_shape=jax.ShapeDtypeStruct((n_slots, d), F32),
        compiler_params=_params("arbitrary", "arbitrary"),
        name="expert_ffn2",
    )(blk_expert, blk_new, act, w2, b2.reshape(n_exp, 1, d))


def _combine_kernel(dest_ref, x_ref, tw_ref, gt_ref, gf_ref, ys_hbm, o_ref, buf, sem, *, rows):
    def row_copy(i):
        return pltpu.make_async_copy(ys_hbm.at[pl.ds(dest_ref[i], 1)],
                                     buf.at[i % TOP_K, pl.ds(i // TOP_K, 1)], sem)

    def start(i, carry):
        row_copy(i).start()
        return carry

    def wait(i, carry):
        row_copy(i).wait()
        return carry

    lax.fori_loop(0, rows * TOP_K, start, 0)
    lax.fori_loop(0, rows * TOP_K, wait, 0)
    tw = tw_ref[...]
    y = buf[0] * tw[:, 0:1]
    for k in range(1, TOP_K):
        y = y + buf[k] * tw[:, k:k + 1]
    x = x_ref[...] + gt_ref[0] * y
    o_ref[...] = x * lax.rsqrt(jnp.mean(x * x, axis=-1, keepdims=True) + EPS) * gf_ref[...]


def _combine(x1, tw, gate, normf_g, ys, dest_flat, seq):
    m, d = x1.shape
    rows = _tile(seq, 128, SUBLANES)
    per_batch = seq // rows
    return pl.pallas_call(
        functools.partial(_combine_kernel, rows=rows),
        grid=(m // rows,),
        in_specs=[pl.BlockSpec((rows * TOP_K,), lambda i: (i,), memory_space=pltpu.SMEM),
                  pl.BlockSpec((rows, d), lambda i: (i, 0)),
                  pl.BlockSpec((rows, LANES), lambda i: (i, 0)),
                  pl.BlockSpec((1, 1, d), lambda i: (i // per_batch, 0, 0)),
                  pl.BlockSpec((1, d), lambda i: (0, 0)),
                  pl.BlockSpec(memory_space=pl.ANY)],
        out_specs=pl.BlockSpec((rows, d), lambda i: (i, 0)),
        out_shape=jax.ShapeDtypeStruct((m, d), F32),
        scratch_shapes=[pltpu.VMEM((TOP_K, rows, d), F32), pltpu.SemaphoreType.DMA(())],
        compiler_params=_params("arbitrary"),
        name="combine",
    )(dest_flat, x1, tw, gate, normf_g.reshape(1, d), ys)


def kernel(x, c, ctx, c_ctx, w_mod, b_mod, norm1_g, w_in, conv_w, a_log, dt_bias, onorm_g, gm_ln_g, gm_ln_b,
           gm_ws, gm_bs, w_up_a, w_up_b, w_o, norm2_g, w_router, b_router, w1, b1, w2, b2, normf_g):
    depth = w_mod.shape[0]
    assert depth == 1, "single-layer block"
    bsz, seq, d = x.shape
    ctx_len = ctx.shape[1]
    n_heads = a_log.shape[-1]
    dn_w = n_heads * HEAD
    gm_w = gm_ln_g.shape[-1]
    n_exp = w_router.shape[-1]
    assert seq % GRID_W == 0 and seq % GM_CHUNK == 0 and seq % DN_CHUNK == 0 and ctx_len % DN_CHUNK == 0
    assert 4 * n_heads <= LANES and GM_CHUNK % (seq // GRID_W) == 0
    (w_mod, b_mod, norm1_g, w_in, conv_w, a_log, dt_bias, onorm_g, gm_ln_g, gm_ln_b, gm_ws, gm_bs, w_up_a, w_up_b,
     w_o, norm2_g, w_router, b_router, w1, b1, w2, b2) = (
        t[0] for t in (w_mod, b_mod, norm1_g, w_in, conv_w, a_log, dt_bias, onorm_g, gm_ln_g, gm_ln_b, gm_ws, gm_bs,
                       w_up_a, w_up_b, w_o, norm2_g, w_router, b_router, w1, b1, w2, b2))

    n_rows = -(-(bsz + 1) // SUBLANES) * SUBLANES
    cond = jnp.concatenate([c, c_ctx[None], jnp.zeros((n_rows - bsz - 1, d), F32)], axis=0)
    mod = _modulation(cond, w_mod, b_mod)
    sh1, sc1, gt1, sh2, sc2, gt2 = (mod[:, i * d:(i + 1) * d].reshape(n_rows, 1, d) for i in range(N_MOD))

    col_decay = 2 * dn_w
    col_q = col_decay + 4 * n_heads
    w_main = jnp.concatenate([w_in[:, :col_decay], w_in[:, col_q:]], axis=1).astype(BF16)
    w_dec = jnp.concatenate([w_in[:, col_decay:col_q], jnp.zeros((d, LANES - 4 * n_heads), F32)], axis=1).astype(BF16)
    col_z = 3 * dn_w
    col_gu = col_z + dn_w
    col_gv = col_gu + gm_w
    col_merge = col_gv + gm_w

    hc = _norm_modulate(ctx, norm1_g, sc1, sh1, lambda i: bsz).reshape(bsz * ctx_len, d)
    pc = _matmul(hc, w_main[:, :2 * dn_w], F32).reshape(bsz, ctx_len, 2 * dn_w)
    rawc = _matmul(hc, w_dec, F32).reshape(bsz, ctx_len, LANES)
    kvc = _short_conv(pc, conv_w, dn_w, 2)
    colsc, dmatc = _decay_prep(rawc, a_log, dt_bias, n_heads)
    zero_state = jnp.zeros((bsz, 2, n_heads, HEAD, HEAD), F32)
    _, _, ctx_state = _delta_scan(kvc, colsc, dmatc, zero_state, n_heads, False)

    h = _norm_modulate(x, norm1_g, sc1, sh1, lambda i: i).reshape(bsz * seq, d)
    p2d = _matmul(h, w_main, F32)
    p = p2d.reshape(bsz, seq, -1)
    raw = _matmul(h, w_dec, F32).reshape(bsz, seq, LANES)
    kvq = _short_conv(p, conv_w, dn_w, 3)
    cols, dmat = _decay_prep(raw, a_log, dt_bias, n_heads)
    o_f, o_b, _ = _delta_scan(kvq, cols, dmat, ctx_state, n_heads, True)
    y_a = _gated_norm(o_f, o_b, p, col_z, onorm_g).reshape(bsz * seq, dn_w)
    gvn = _gelu_layernorm(p, col_gv, gm_w, gm_ln_g, gm_ln_b)
    y_b = _spatial_gate(gvn, p, col_gu, gm_ws, gm_bs).reshape(bsz * seq, gm_w)
    merged = _merge(y_a, y_b, w_up_a.astype(BF16), w_up_b.astype(BF16), p2d, col_merge)
    x2d = x.reshape(bsz * seq, d)
    x1 = _out_proj_residual(merged, w_o.astype(BF16), x2d, gt1, seq)

    h2, ti, tw, rk, cnt = _router(x1, norm2_g, sc2, sh2, w_router, b_router, seq)
    n_tok = bsz * seq
    top_i = ti[:, :TOP_K]
    counts = cnt[0].astype(I32)
    padded = (counts + MOE_ROWS - 1) // MOE_ROWS * MOE_ROWS
    pad_end = jnp.cumsum(padded)
    pad_start = pad_end - padded
    dest = (pad_start[top_i] + rk[:, :TOP_K]).reshape(-1)
    n_slots = n_tok * TOP_K + n_exp * MOE_ROWS
    tok_flat = jnp.repeat(jnp.arange(n_tok, dtype=I32), TOP_K)
    slot_tok = jnp.zeros((n_slots,), I32).at[dest].set(tok_flat)
    blk_start = jnp.arange(n_slots // MOE_ROWS, dtype=I32) * MOE_ROWS
    blk_expert = jnp.minimum(jnp.searchsorted(pad_end, blk_start, side="right"), n_exp - 1).astype(I32)
    blk_new = jnp.concatenate([jnp.ones((1,), I32), (blk_expert[1:] != blk_expert[:-1]).astype(I32)])
    xs = _dispatch(h2, slot_tok)
    act = _ffn1(xs, w1, b1, blk_expert, blk_new)
    ys = _ffn2(act, w2, b2, blk_expert, blk_new)
    out = _combine(x1, tw, gt2, normf_g, ys, dest, seq)
    return out.reshape(bsz, seq, d)
```

```python
import functools

import jax
import jax.numpy as jnp
from jax import lax
from jax.experimental import pallas as pl
from jax.experimental.pallas import tpu as pltpu

F32 = jnp.float32
BF16 = jnp.bfloat16
I32 = jnp.int32
HIGHEST = lax.Precision.HIGHEST

EPS = 1e-6
N_MOD = 6
GRID_W = 64
HEAD = 128
DN_CHUNK = 64
GM_CHUNK = 128
TOP_K = 4
SWIGLU_LIMIT = 7.0
SWIGLU_ALPHA = 1.702
MOE_ROWS = 256
BLK_SAME_EXPERT, BLK_NEW_EXPERT, BLK_UNUSED = 0, 1, 2
LANES = 128
SUBLANES = 8
VMEM_LIMIT = 56 * 1024 * 1024

NT_DIMS = (((1,), (1,)), ((), ()))
TN_DIMS = (((0,), (0,)), ((), ()))


def _tile(dim, pref, align=LANES):
    t = min(pref, dim) // align * align
    while t >= align:
        if dim % t == 0:
            return t
        t -= align
    return dim


def _params(*sem):
    return pltpu.CompilerParams(dimension_semantics=sem, vmem_limit_bytes=VMEM_LIMIT)


def _silu(x):
    return x * jax.nn.sigmoid(x)


def _gelu_tanh(x):
    c = 0.7978845608028654
    return 0.5 * x * (1.0 + jnp.tanh(c * (x + 0.044715 * (x * x * x))))


def _mod_kernel(c_ref, w_ref, b_ref, o_ref):
    s = _silu(c_ref[...]).astype(BF16)
    o_ref[...] = jnp.dot(s, w_ref[...].astype(BF16), preferred_element_type=F32) + b_ref[...]


def _modulation(cond, w_mod, b_mod):
    rows, d = cond.shape
    n = w_mod.shape[1]
    tn = _tile(n, 512)
    return pl.pallas_call(
        _mod_kernel,
        grid=(n // tn,),
        in_specs=[pl.BlockSpec((rows, d), lambda j: (0, 0)),
                  pl.BlockSpec((d, tn), lambda j: (0, j)),
                  pl.BlockSpec((1, tn), lambda j: (0, j))],
        out_specs=pl.BlockSpec((rows, tn), lambda j: (0, j)),
        out_shape=jax.ShapeDtypeStruct((rows, n), F32),
        compiler_params=_params("parallel"),
        name="modulation",
    )(cond, w_mod, b_mod.reshape(1, n))


def _rms_mod(x, g, sc, sh):
    y = x * lax.rsqrt(jnp.mean(x * x, axis=-1, keepdims=True) + EPS)
    return (y * g) * (1.0 + sc) + sh


def _normmod_kernel(x_ref, g_ref, sc_ref, sh_ref, o_ref):
    o_ref[0] = _rms_mod(x_ref[0], g_ref[...], sc_ref[0], sh_ref[0]).astype(o_ref.dtype)


def _norm_modulate(x, g, sc, sh, row_of_batch):
    b, l, d = x.shape
    tl = _tile(l, 512, SUBLANES)
    return pl.pallas_call(
        _normmod_kernel,
        grid=(b, l // tl),
        in_specs=[pl.BlockSpec((1, tl, d), lambda i, j: (i, j, 0)),
                  pl.BlockSpec((1, d), lambda i, j: (0, 0)),
                  pl.BlockSpec((1, 1, d), lambda i, j: (row_of_batch(i), 0, 0)),
                  pl.BlockSpec((1, 1, d), lambda i, j: (row_of_batch(i), 0, 0))],
        out_specs=pl.BlockSpec((1, tl, d), lambda i, j: (i, j, 0)),
        out_shape=jax.ShapeDtypeStruct((b, l, d), BF16),
        compiler_params=_params("parallel", "parallel"),
        name="norm_modulate",
    )(x, g.reshape(1, d), sc, sh)


def _mm_kernel(a_ref, w_ref, o_ref):
    o_ref[...] = jnp.dot(a_ref[...], w_ref[...], preferred_element_type=F32).astype(o_ref.dtype)


def _matmul(a, w, out_dtype, tm_pref=1024, tn_pref=1024):
    m, k = a.shape
    n = w.shape[1]
    tm = _tile(m, tm_pref, SUBLANES)
    tn = _tile(n, tn_pref)
    return pl.pallas_call(
        _mm_kernel,
        grid=(m // tm, n // tn),
        in_specs=[pl.BlockSpec((tm, k), lambda i, j: (i, 0)),
                  pl.BlockSpec((k, tn), lambda i, j: (0, j))],
        out_specs=pl.BlockSpec((tm, tn), lambda i, j: (i, j)),
        out_shape=jax.ShapeDtypeStruct((m, n), out_dtype),
        compiler_params=_params("parallel", "parallel"),
        name="matmul",
    )(a, w)


def _conv_kernel(prev_ref, cur_ref, next_ref, w_ref, o_ref, *, tl, n_t, tc, k_tiles, conv_k):
    i = pl.program_id(1)
    j = pl.program_id(2)
    prev = jnp.where(i > 0, prev_ref[0], 0.0)
    nxt = jnp.where(i < n_t - 1, next_ref[0], 0.0)
    xx = jnp.concatenate([prev, cur_ref[0], nxt], axis=0)
    n = tl + 2 * SUBLANES
    acc = None
    for jj in range(conv_k):
        s = jj - conv_k // 2
        shifted = xx if s == 0 else pltpu.roll(xx, (-s) % n, 0)
        term = shifted[SUBLANES:SUBLANES + tl] * w_ref[jj:jj + 1, :]
        acc = term if acc is None else acc + term
    y = _silu(acc)
    is_k = j < k_tiles
    is_q = j >= 2 * k_tiles
    outs = []
    for hh in range(tc // HEAD):
        yh = y[:, hh * HEAD:(hh + 1) * HEAD]
        inv = lax.rsqrt(jnp.sum(yh * yh, axis=-1, keepdims=True) + EPS)
        scale = jnp.where(is_k, inv, jnp.where(is_q, inv * (HEAD ** -0.5), 1.0))
        outs.append(yh * scale)
    o_ref[0] = jnp.concatenate(outs, axis=-1)


def _short_conv(p, conv_w, width, n_sections):
    b, l, _ = p.shape
    conv_k = conv_w.shape[0]
    c = n_sections * width
    tc = _tile(width, 512)
    tl = _tile(l, 512, SUBLANES)
    n_t = l // tl
    sub = tl // SUBLANES
    kern = functools.partial(_conv_kernel, tl=tl, n_t=n_t, tc=tc, k_tiles=width // tc, conv_k=conv_k)
    return pl.pallas_call(
        kern,
        grid=(b, n_t, c // tc),
        in_specs=[
            pl.BlockSpec((1, SUBLANES, tc), lambda bi, i, j: (bi, jnp.maximum(i * sub - 1, 0), j)),
            pl.BlockSpec((1, tl, tc), lambda bi, i, j: (bi, i, j)),
            pl.BlockSpec((1, SUBLANES, tc), lambda bi, i, j: (bi, jnp.minimum((i + 1) * sub, l // SUBLANES - 1), j)),
            pl.BlockSpec((conv_k, tc), lambda bi, i, j: (0, j)),
        ],
        out_specs=pl.BlockSpec((1, tl, tc), lambda bi, i, j: (bi, i, j)),
        out_shape=jax.ShapeDtypeStruct((b, l, c), F32),
        compiler_params=_params("parallel", "parallel", "parallel"),
        name="short_conv",
    )(p, p, p, conv_w[:, :c])


def _decay_kernel(raw_ref, alog_ref, dtb_ref, cols_ref, dmat_ref, *, n_heads):
    h = n_heads
    c = DN_CHUNK
    raw = raw_ref[0]
    x = raw + dtb_ref[...]
    softplus = jnp.maximum(x, 0.0) + jnp.log1p(jnp.exp(-jnp.abs(x)))
    gs = -jnp.exp(alog_ref[...]) * softplus
    beta = jax.nn.sigmoid(raw)
    ii = lax.broadcasted_iota(I32, (c, c), 0)
    jj = lax.broadcasted_iota(I32, (c, c), 1)
    tri = [jj <= ii, jj >= ii]
    trif = [t.astype(F32) for t in tri]
    eye = (lax.broadcasted_iota(I32, (LANES, LANES), 0) == lax.broadcasted_iota(I32, (LANES, LANES), 1)).astype(F32)
    gs_t = lax.dot_general(eye, gs, NT_DIMS, precision=HIGHEST, preferred_element_type=F32)
    lane = lax.broadcasted_iota(I32, (c, LANES), 1)
    g_dir = [jnp.dot(trif[d], gs, precision=HIGHEST, preferred_element_type=F32) for d in range(2)]
    g_cum = jnp.where(lane < h, g_dir[0], g_dir[1])
    g_last = jnp.where(lane < h, g_dir[0][c - 1:c, :], g_dir[1][0:1, :])
    cols_ref[0] = jnp.concatenate([beta, jnp.exp(g_cum), jnp.exp(g_last - g_cum), jnp.exp(g_last)], axis=-1)
    for d in range(2):
        g_row = lax.dot_general(gs_t, trif[d], NT_DIMS, precision=HIGHEST, preferred_element_type=F32)
        mats = []
        for hh in range(h):
            col = d * h + hh
            diff = g_dir[d][:, col:col + 1] - g_row[col:col + 1, :]
            mats.append(jnp.where(tri[d], jnp.exp(jnp.where(tri[d], diff, 0.0)), 0.0))
        dmat_ref[0, d] = jnp.concatenate(mats, axis=-1)


def _decay_prep(raw, a_log, dt_bias, n_heads):
    b, l, _ = raw.shape
    h = n_heads
    pad = LANES - 2 * h
    alog = jnp.concatenate([a_log.reshape(1, 2 * h), jnp.zeros((1, pad), F32)], axis=-1)
    dtb = jnp.concatenate([dt_bias.reshape(1, 2 * h), jnp.zeros((1, pad), F32)], axis=-1)
    n = l // DN_CHUNK
    return pl.pallas_call(
        functools.partial(_decay_kernel, n_heads=h),
        grid=(b, n),
        in_specs=[pl.BlockSpec((1, DN_CHUNK, LANES), lambda i, j: (i, j, 0)),
                  pl.BlockSpec((1, LANES), lambda i, j: (0, 0)),
                  pl.BlockSpec((1, LANES), lambda i, j: (0, 0))],
        out_specs=[pl.BlockSpec((1, DN_CHUNK, 4 * LANES), lambda i, j: (i, j, 0)),
                   pl.BlockSpec((1, 2, DN_CHUNK, h * DN_CHUNK), lambda i, j: (i, 0, j, 0))],
        out_shape=[jax.ShapeDtypeStruct((b, l, 4 * LANES), F32),
                   jax.ShapeDtypeStruct((b, 2, l, h * DN_CHUNK), F32)],
        compiler_params=_params("parallel", "parallel"),
        name="decay_prep",
    )(raw, alog, dtb)


def _dot16(a, b, dims=None):
    a = a.astype(BF16)
    b = b.astype(BF16)
    if dims is None:
        return jnp.dot(a, b, preferred_element_type=F32)
    return lax.dot_general(a, b, dims, preferred_element_type=F32)


def _unit_triangular_inverses(mats, ii, jj):
    eye = (ii == jj).astype(F32)
    blk8 = (ii // 8) == (jj // 8)
    a0 = [jnp.where(blk8, a, 0.0) for a in mats]
    a2 = [_dot16(x, x) for x in a0]
    a4 = [_dot16(x, x) for x in a2]
    p = [eye - x for x in a0]
    p = [x + _dot16(x, y) for x, y in zip(p, a2)]
    p = [x + _dot16(x, y) for x, y in zip(p, a4)]
    for s in (8, 16, 32):
        sel = ((ii // (2 * s)) == (jj // (2 * s))) & ((ii // s) != (jj // s))
        t = [_dot16(jnp.where(sel, a, 0.0), x) for a, x in zip(mats, p)]
        p = [x - _dot16(x, y) for x, y in zip(p, t)]
    return p


def _delta_kernel(*refs, hb, n_heads, with_q, cps):
    if with_q:
        (kf_ref, vf_ref, qf_ref, kb_ref, vb_ref, qb_ref, df_ref, db_ref, cf_ref, cb_ref, s0_ref,
         of_ref, ob_ref, sfin_ref, s_ref) = refs
        q_refs = (qf_ref, qb_ref)
        o_refs = (of_ref, ob_ref)
    else:
        (kf_ref, vf_ref, kb_ref, vb_ref, df_ref, db_ref, cf_ref, cb_ref, s0_ref, sfin_ref, s_ref) = refs
    k_refs = (kf_ref, kb_ref)
    v_refs = (vf_ref, vb_ref)
    d_refs = (df_ref, db_ref)
    c_refs = (cf_ref, cb_ref)
    hg = pl.program_id(1)
    n = pl.program_id(2)
    c = DN_CHUNK

    @pl.when(n == 0)
    def _():
        s_ref[...] = s0_ref[0]

    ii = lax.broadcasted_iota(I32, (c, c), 0)
    jj = lax.broadcasted_iota(I32, (c, c), 1)
    lane = lax.broadcasted_iota(I32, (c, LANES), 1)

    def column(x, idx):
        return jnp.sum(jnp.where(lane == idx, x, 0.0), axis=-1, keepdims=True)

    chains = [(d, hh) for d in range(2) for hh in range(hb)]
    items = [(d, hh, ci) for ci in range(cps) for d, hh in chains]

    def rows_of(d, ci):
        r0 = (ci if d == 0 else cps - 1 - ci) * c
        return slice(r0, r0 + c)

    beta, eg, ekend, eglast, k, v, dm = [], [], [], [], [], [], []
    for d, hh, ci in items:
        head = hg * hb + hh
        cols = c_refs[d][0][rows_of(d, ci), :]
        beta.append(column(cols[:, 0:LANES], (2 + d) * n_heads + head))
        eg.append(column(cols[:, LANES:2 * LANES], d * n_heads + head))
        ekend.append(column(cols[:, 2 * LANES:3 * LANES], d * n_heads + head))
        eglast.append(column(cols[:, 3 * LANES:4 * LANES], d * n_heads + head)[0:1, :])
        k.append(k_refs[d][0][rows_of(d, ci), hh * HEAD:(hh + 1) * HEAD])
        v.append(v_refs[d][0][rows_of(d, ci), hh * HEAD:(hh + 1) * HEAD])
        dm.append(d_refs[d][0, 0][rows_of(d, ci), hh * c:(hh + 1) * c])
    strict = [(jj < ii) if d == 0 else (jj > ii) for d, _, _ in items]
    kb = [x * y for x, y in zip(k, beta)]
    kk = [_dot16(x, y, NT_DIMS) for x, y in zip(kb, k)]
    a = [jnp.where(m, x * y, 0.0) for m, x, y in zip(strict, kk, dm)]
    t = _unit_triangular_inverses(a, ii, jj)
    u = [_dot16(x, y * z) for x, y, z in zip(t, v, beta)]
    w = [_dot16(x, y * z) for x, y, z in zip(t, kb, eg)]
    ke = [x * y for x, y in zip(k, ekend)]
    if with_q:
        q = [q_refs[d][0][rows_of(d, ci), hh * HEAD:(hh + 1) * HEAD] for d, hh, ci in items]
        qk = [_dot16(x, y, NT_DIMS) * z for x, y, z in zip(q, k, dm)]
        qd = [x * y for x, y in zip(q, eg)]
        o = {}
    s = [s_ref[d, hh] for d, hh in chains]
    for ci in range(cps):
        idx = [ci * len(chains) + n_c for n_c in range(len(chains))]
        v_new = [u[i] - _dot16(w[i], x) for i, x in zip(idx, s)]
        if with_q:
            for i, x, vn, (d, hh) in zip(idx, s, v_new, chains):
                o[(d, hh, ci)] = _dot16(qd[i], x) + _dot16(qk[i], vn)
        upd = [_dot16(ke[i], vn, TN_DIMS) for i, vn in zip(idx, v_new)]
        s = [x * eglast[i] + z for i, x, z in zip(idx, s, upd)]
    for (d, hh), x in zip(chains, s):
        s_ref[d, hh] = x
    if with_q:
        for d in range(2):
            order = range(cps) if d == 0 else range(cps - 1, -1, -1)
            o_refs[d][0] = jnp.concatenate(
                [jnp.concatenate([o[(d, hh, ci)] for hh in range(hb)], axis=-1) for ci in order], axis=0)

    @pl.when(n == pl.num_programs(2) - 1)
    def _():
        sfin_ref[0] = s_ref[...]


def _delta_scan(kvq, cols, dmat, s0, n_heads, with_q):
    b, l, _ = kvq.shape
    h = n_heads
    hb = 4 if h % 4 == 0 else (2 if h % 2 == 0 else 1)
    n_chunks = l // DN_CHUNK
    cps = 4 if n_chunks % 4 == 0 else (2 if n_chunks % 2 == 0 else 1)
    c = cps * DN_CHUNK
    n = l // c
    wblk = h // hb

    def sec(section, rev):
        def imap(bi, g, j):
            return (bi, (n - 1 - j) if rev else j, section * wblk + g)
        return pl.BlockSpec((1, c, hb * HEAD), imap)

    def dspec(d):
        return pl.BlockSpec((1, 1, c, hb * DN_CHUNK), lambda bi, g, j: (bi, d, (n - 1 - j) if d else j, g))

    def cspec(d):
        return pl.BlockSpec((1, c, 4 * LANES), lambda bi, g, j: (bi, (n - 1 - j) if d else j, 0))

    state_spec = pl.BlockSpec((1, 2, hb, HEAD, HEAD), lambda bi, g, j: (bi, 0, g, 0, 0))
    n_sec = 3 if with_q else 2
    in_specs = [sec(s, False) for s in range(n_sec)] + [sec(s, True) for s in range(n_sec)]
    in_specs += [dspec(0), dspec(1), cspec(0), cspec(1), state_spec]
    args = [kvq] * (2 * n_sec) + [dmat, dmat, cols, cols, s0]
    out_specs = [state_spec]
    out_shape = [jax.ShapeDtypeStruct((b, 2, h, HEAD, HEAD), F32)]
    if with_q:
        ospec = [pl.BlockSpec((1, c, hb * HEAD), lambda bi, g, j: (bi, j, g)),
                 pl.BlockSpec((1, c, hb * HEAD), lambda bi, g, j: (bi, n - 1 - j, g))]
        out_specs = ospec + out_specs
        out_shape = [jax.ShapeDtypeStruct((b, l, h * HEAD), F32)] * 2 + out_shape
    res = pl.pallas_call(
        functools.partial(_delta_kernel, hb=hb, n_heads=h, with_q=with_q, cps=cps),
        grid=(b, wblk, n),
        in_specs=in_specs,
        out_specs=out_specs,
        out_shape=out_shape,
        scratch_shapes=[pltpu.VMEM((2, hb, HEAD, HEAD), F32)],
        compiler_params=_params("parallel", "parallel", "arbitrary"),
        name="delta_scan_q" if with_q else "delta_scan_state",
    )(*args)
    if with_q:
        return res[0], res[1], res[2]
    return None, None, res[0]


def _gated_norm_kernel(of_ref, ob_ref, z_ref, g_ref, o_ref):
    o = of_ref[0] + ob_ref[0]
    z = z_ref[0]
    outs = []
    for hh in range(o.shape[-1] // HEAD):
        oh = o[:, hh * HEAD:(hh + 1) * HEAD]
        y = oh * lax.rsqrt(jnp.mean(oh * oh, axis=-1, keepdims=True) + EPS) * g_ref[...]
        outs.append(y * _silu(z[:, hh * HEAD:(hh + 1) * HEAD]))
    o_ref[0] = jnp.concatenate(outs, axis=-1).astype(o_ref.dtype)


def _gated_norm(o_f, o_b, p, z_col, onorm_g):
    b, l, w = o_f.shape
    tl = _tile(l, 512, SUBLANES)
    tc = _tile(w, 512)
    zb = z_col // tc
    return pl.pallas_call(
        _gated_norm_kernel,
        grid=(b, l // tl, w // tc),
        in_specs=[pl.BlockSpec((1, tl, tc), lambda i, j, k: (i, j, k)),
                  pl.BlockSpec((1, tl, tc), lambda i, j, k: (i, j, k)),
                  pl.BlockSpec((1, tl, tc), lambda i, j, k: (i, j, zb + k)),
                  pl.BlockSpec((1, HEAD), lambda i, j, k: (0, 0))],
        out_specs=pl.BlockSpec((1, tl, tc), lambda i, j, k: (i, j, k)),
        out_shape=jax.ShapeDtypeStruct((b, l, w), BF16),
        compiler_params=_params("parallel", "parallel", "parallel"),
        name="gated_norm",
    )(o_f, o_b, p, onorm_g.reshape(1, HEAD))


def _gelu_ln_kernel(x_ref, g_ref, b_ref, o_ref):
    x = _gelu_tanh(x_ref[0])
    mu = jnp.mean(x, axis=-1, keepdims=True)
    xc = x - mu
    var = jnp.mean(xc * xc, axis=-1, keepdims=True)
    o_ref[0] = (xc * lax.rsqrt(var + EPS)) * g_ref[...] + b_ref[...]


def _gelu_layernorm(p, col, width, g, bias):
    b, l, _ = p.shape
    tl = _tile(l, 256, SUBLANES)
    cb = col // width
    return pl.pallas_call(
        _gelu_ln_kernel,
        grid=(b, l // tl),
        in_specs=[pl.BlockSpec((1, tl, width), lambda i, j: (i, j, cb)),
                  pl.BlockSpec((1, width), lambda i, j: (0, 0)),
                  pl.BlockSpec((1, width), lambda i, j: (0, 0))],
        out_specs=pl.BlockSpec((1, tl, width), lambda i, j: (i, j, 0)),
        out_shape=jax.ShapeDtypeStruct((b, l, width), F32),
        compiler_params=_params("parallel", "parallel"),
        name="gelu_layernorm",
    )(p, g.reshape(1, width), bias.reshape(1, width))


def _spatial_kernel(gv_ref, gu_ref, ws_ref, bs_ref, o_ref, *, seq, row_groups, grid_w):
    g = pl.program_id(1)
    w = ws_ref[0].astype(BF16)
    bias = bs_ref[0]
    n_chunks = seq // GM_CHUNK
    rows = seq // grid_w
    cols_per_chunk = GM_CHUNK // rows

    @pl.when(g < row_groups)
    def _():
        for n in range(n_chunks):
            sl = pl.ds(n * GM_CHUNK, GM_CHUNK)
            s = jnp.dot(w, gv_ref[0, sl, :].astype(BF16), preferred_element_type=F32) + bias
            o_ref[0, sl, :] = (_gelu_tanh(gu_ref[0, sl, :]) * s).astype(o_ref.dtype)

    @pl.when(g >= row_groups)
    def _():
        for n in range(n_chunks):
            sls = [pl.ds(n * cols_per_chunk + cc, rows, stride=grid_w) for cc in range(cols_per_chunk)]
            v = jnp.concatenate([gv_ref[0, sl, :] for sl in sls], axis=0)
            s = jnp.dot(w, v.astype(BF16), preferred_element_type=F32) + bias
            for cc, sl in enumerate(sls):
                o_ref[0, sl, :] = (_gelu_tanh(gu_ref[0, sl, :]) * s[cc * rows:(cc + 1) * rows]).astype(o_ref.dtype)


def _spatial_gate(gvn, p, gu_col, gm_ws, gm_bs):
    b, l, width = gvn.shape
    groups = gm_ws.shape[0]
    gub = gu_col // HEAD
    kern = functools.partial(_spatial_kernel, seq=l, row_groups=groups // 2, grid_w=GRID_W)
    return pl.pallas_call(
        kern,
        grid=(b, groups),
        in_specs=[pl.BlockSpec((1, l, HEAD), lambda i, g: (i, 0, g)),
                  pl.BlockSpec((1, l, HEAD), lambda i, g: (i, 0, gub + g)),
                  pl.BlockSpec((1, GM_CHUNK, GM_CHUNK), lambda i, g: (g, 0, 0)),
                  pl.BlockSpec((1, GM_CHUNK, 1), lambda i, g: (g, 0, 0))],
        out_specs=pl.BlockSpec((1, l, HEAD), lambda i, g: (i, 0, g)),
        out_shape=jax.ShapeDtypeStruct((b, l, width), F32),
        compiler_params=_params("parallel", "parallel"),
        name="spatial_gate",
    )(gvn, p, gm_ws, gm_bs.reshape(groups, GM_CHUNK, 1))


def _merge_kernel(ya_ref, yb_ref, wa_ref, wb_ref, ga_ref, gb_ref, o_ref):
    pa = jnp.dot(ya_ref[...], wa_ref[...], preferred_element_type=F32)
    pb = jnp.dot(yb_ref[...].astype(BF16), wb_ref[...], preferred_element_type=F32)
    o_ref[...] = (jax.nn.sigmoid(ga_ref[...]) * pa + jax.nn.sigmoid(gb_ref[...]) * pb).astype(o_ref.dtype)


def _merge(ya, yb, wa, wb, p2d, merge_col):
    m, ka = ya.shape
    kb = yb.shape[1]
    d = wa.shape[1]
    tm = _tile(m, 512, SUBLANES)
    tn = _tile(d, 1024)
    ca = merge_col // tn
    cb = (merge_col + d) // tn
    return pl.pallas_call(
        _merge_kernel,
        grid=(m // tm, d // tn),
        in_specs=[pl.BlockSpec((tm, ka), lambda i, j: (i, 0)),
                  pl.BlockSpec((tm, kb), lambda i, j: (i, 0)),
                  pl.BlockSpec((ka, tn), lambda i, j: (0, j)),
                  pl.BlockSpec((kb, tn), lambda i, j: (0, j)),
                  pl.BlockSpec((tm, tn), lambda i, j: (i, ca + j)),
                  pl.BlockSpec((tm, tn), lambda i, j: (i, cb + j))],
        out_specs=pl.BlockSpec((tm, tn), lambda i, j: (i, j)),
        out_shape=jax.ShapeDtypeStruct((m, d), BF16),
        compiler_params=_params("parallel", "parallel"),
        name="merge",
    )(ya, yb, wa, wb, p2d, p2d)


def _oproj_kernel(a_ref, w_ref, x_ref, gt_ref, o_ref):
    y = jnp.dot(a_ref[...], w_ref[...], preferred_element_type=F32)
    o_ref[...] = x_ref[...] + gt_ref[0] * y


def _out_proj_residual(a, w, x2d, gate, seq):
    m, k = a.shape
    d = w.shape[1]
    tm = _tile(seq, 1024, SUBLANES)
    tn = _tile(d, 1024)
    per_batch = seq // tm
    return pl.pallas_call(
        _oproj_kernel,
        grid=(m // tm, d // tn),
        in_specs=[pl.BlockSpec((tm, k), lambda i, j: (i, 0)),
                  pl.BlockSpec((k, tn), lambda i, j: (0, j)),
                  pl.BlockSpec((tm, tn), lambda i, j: (i, j)),
                  pl.BlockSpec((1, 1, tn), lambda i, j: (i // per_batch, 0, j))],
        out_specs=pl.BlockSpec((tm, tn), lambda i, j: (i, j)),
        out_shape=jax.ShapeDtypeStruct((m, d), F32),
        compiler_params=_params("parallel", "parallel"),
        name="out_proj",
    )(a, w, x2d, gate)


def _router_kernel(x_ref, g_ref, sc_ref, sh_ref, wr_ref, br_ref, h_ref, ti_ref, tw_ref, rk_ref, cnt_ref, carry,
                   *, n_exp, tm):
    i = pl.program_id(0)

    @pl.when(i == 0)
    def _():
        carry[...] = jnp.zeros_like(carry)

    h2 = _rms_mod(x_ref[...], g_ref[...], sc_ref[0], sh_ref[0])
    cpr = h2.shape[1] // LANES
    for col in range(cpr):
        h_ref[pl.ds(col, tm, stride=cpr), :] = h2[:, col * LANES:(col + 1) * LANES]
    h_hi = h2.astype(BF16)
    h_lo = (h2 - h_hi.astype(F32)).astype(BF16)
    wr = wr_ref[...]
    w_hi = wr.astype(BF16)
    w_lo = (wr - w_hi.astype(F32)).astype(BF16)
    logits = (jnp.dot(h_hi, w_hi, preferred_element_type=F32) + jnp.dot(h_hi, w_lo, preferred_element_type=F32)
              + jnp.dot(h_lo, w_hi, preferred_element_type=F32)) + br_ref[...]
    lane = lax.broadcasted_iota(I32, (tm, n_exp), 1)
    work = logits
    member = jnp.zeros((tm, n_exp), F32)
    vals, ids = [], []
    for _ in range(TOP_K):
        mx = jnp.max(work, axis=-1, keepdims=True)
        idx = jnp.min(jnp.where(work == mx, lane, n_exp), axis=-1, keepdims=True)
        hit = lane == idx
        vals.append(mx)
        ids.append(idx)
        work = jnp.where(hit, -jnp.inf, work)
        member = member + hit.astype(F32)
    exps = [jnp.exp(v - vals[0]) for v in vals]
    den = exps[0]
    for e in exps[1:]:
        den = den + e
    rr = lax.broadcasted_iota(I32, (tm, tm), 0)
    cc = lax.broadcasted_iota(I32, (tm, tm), 1)
    below = (cc < rr).astype(BF16)
    rank_all = jnp.dot(below, member.astype(BF16), preferred_element_type=F32) + carry[...]
    out_lane = lax.broadcasted_iota(I32, (tm, LANES), 1)
    ti = jnp.zeros((tm, LANES), I32)
    tw = jnp.zeros((tm, LANES), F32)
    rk = jnp.zeros((tm, LANES), F32)
    for k in range(TOP_K):
        rank_k = jnp.sum(jnp.where(lane == ids[k], rank_all, 0.0), axis=-1, keepdims=True)
        ti = jnp.where(out_lane == k, ids[k], ti)
        tw = jnp.where(out_lane == k, exps[k] / den, tw)
        rk = jnp.where(out_lane == k, rank_k, rk)
    ti_ref[...] = ti
    tw_ref[...] = tw
    rk_ref[...] = rk.astype(I32)
    carry[...] = carry[...] + jnp.sum(member, axis=0, keepdims=True)
    cnt_ref[...] = carry[...]


def _router(x2d, g, sc, sh, w_router, b_router, seq):
    m, d = x2d.shape
    n_exp = w_router.shape[1]
    tm = _tile(seq, 256, SUBLANES)
    per_batch = seq // tm
    kern = functools.partial(_router_kernel, n_exp=n_exp, tm=tm)
    row = lambda i: (i, 0)
    return pl.pallas_call(
        kern,
        grid=(m // tm,),
        in_specs=[pl.BlockSpec((tm, d), row),
                  pl.BlockSpec((1, d), lambda i: (0, 0)),
                  pl.BlockSpec((1, 1, d), lambda i: (i // per_batch, 0, 0)),
                  pl.BlockSpec((1, 1, d), lambda i: (i // per_batch, 0, 0)),
                  pl.BlockSpec((d, n_exp), lambda i: (0, 0)),
                  pl.BlockSpec((1, n_exp), lambda i: (0, 0))],
        out_specs=[pl.BlockSpec((tm * (d // LANES), LANES), row),
                   pl.BlockSpec((tm, LANES), row),
                   pl.BlockSpec((tm, LANES), row),
                   pl.BlockSpec((tm, LANES), row),
                   pl.BlockSpec((1, n_exp), lambda i: (0, 0))],
        out_shape=[jax.ShapeDtypeStruct((m * (d // LANES), LANES), F32),
                   jax.ShapeDtypeStruct((m, LANES), I32),
                   jax.ShapeDtypeStruct((m, LANES), F32),
                   jax.ShapeDtypeStruct((m, LANES), I32),
                   jax.ShapeDtypeStruct((1, n_exp), F32)],
        scratch_shapes=[pltpu.VMEM((1, n_exp), F32)],
        compiler_params=_params("arbitrary"),
        name="router",
    )(x2d, g.reshape(1, d), sc, sh, w_router, b_router.reshape(1, n_exp))


def _dispatch_kernel(state_ref, tok_ref, h_hbm, xs_ref, buf, sem, *, rows, cpr):
    live = state_ref[pl.program_id(0)] != BLK_UNUSED

    @pl.when(live)
    def _():
        def start(i, carry):
            src = h_hbm.at[pl.ds(pl.multiple_of(tok_ref[i] * cpr, cpr), cpr)]
            pltpu.make_async_copy(src, buf.at[pl.ds(pl.multiple_of(i * cpr, cpr), cpr)], sem).start()
            return carry

        lax.fori_loop(0, rows, start, 0, unroll=8)
        pltpu.make_async_copy(h_hbm.at[pl.ds(0, rows * cpr)], buf, sem).wait()
        for col in range(cpr):
            xs_ref[:, col * LANES:(col + 1) * LANES] = buf[pl.ds(col, rows, stride=cpr), :].astype(xs_ref.dtype)

    @pl.when(jnp.logical_not(live))
    def _():
        xs_ref[...] = jnp.zeros_like(xs_ref)


def _dispatch(h2_slabs, slot_tok, blk_state, d):
    n_slots = slot_tok.shape[0]
    rows = MOE_ROWS
    cpr = d // LANES
    grid_spec = pltpu.PrefetchScalarGridSpec(
        num_scalar_prefetch=1,
        grid=(n_slots // rows,),
        in_specs=[pl.BlockSpec((rows,), lambda i, st: (i,), memory_space=pltpu.SMEM),
                  pl.BlockSpec(memory_space=pl.ANY)],
        out_specs=pl.BlockSpec((rows, d), lambda i, st: (i, 0)),
        scratch_shapes=[pltpu.VMEM((rows * cpr, LANES), h2_slabs.dtype), pltpu.SemaphoreType.DMA(())],
    )
    return pl.pallas_call(
        functools.partial(_dispatch_kernel, rows=rows, cpr=cpr),
        grid_spec=grid_spec,
        out_shape=jax.ShapeDtypeStruct((n_slots, d), BF16),
        compiler_params=_params("arbitrary"),
        name="dispatch",
    )(blk_state, slot_tok, h2_slabs)


def _ffn1_kernel(be_ref, state_ref, xs_ref, wg_ref, wl_ref, bg_ref, bl_ref, o_ref, wg16, wl16):
    del be_ref
    state = state_ref[pl.program_id(1)]

    @pl.when(state == BLK_NEW_EXPERT)
    def _():
        wg16[...] = wg_ref[0].astype(BF16)
        wl16[...] = wl_ref[0].astype(BF16)

    @pl.when(state != BLK_UNUSED)
    def _():
        x = xs_ref[...]
        glu = jnp.dot(x, wg16[...], preferred_element_type=F32) + bg_ref[0]
        lin = jnp.dot(x, wl16[...], preferred_element_type=F32) + bl_ref[0]
        glu = jnp.minimum(glu, SWIGLU_LIMIT)
        lin = jnp.clip(lin, -SWIGLU_LIMIT, SWIGLU_LIMIT)
        o_ref[...] = (glu * jax.nn.sigmoid(SWIGLU_ALPHA * glu) * (lin + 1.0)).astype(o_ref.dtype)

    @pl.when(state == BLK_UNUSED)
    def _():
        o_ref[...] = jnp.zeros_like(o_ref)


def _ffn1(xs, w1, b1, blk_expert, blk_new):
    n_slots, d = xs.shape
    n_exp, _, two_f = w1.shape
    f = two_f // 2
    rows = MOE_ROWS
    tn = _tile(f, 512)
    nj = f // tn
    grid_spec = pltpu.PrefetchScalarGridSpec(
        num_scalar_prefetch=2,
        grid=(nj, n_slots // rows),
        in_specs=[pl.BlockSpec((rows, d), lambda j, i, be, nw: (i, 0)),
                  pl.BlockSpec((1, d, tn), lambda j, i, be, nw: (be[i], 0, j)),
                  pl.BlockSpec((1, d, tn), lambda j, i, be, nw: (be[i], 0, nj + j)),
                  pl.BlockSpec((1, 1, tn), lambda j, i, be, nw: (be[i], 0, j)),
                  pl.BlockSpec((1, 1, tn), lambda j, i, be, nw: (be[i], 0, nj + j))],
        out_specs=pl.BlockSpec((rows, tn), lambda j, i, be, nw: (i, j)),
        scratch_shapes=[pltpu.VMEM((d, tn), BF16), pltpu.VMEM((d, tn), BF16)],
    )
    return pl.pallas_call(
        _ffn1_kernel,
        grid_spec=grid_spec,
        out_shape=jax.ShapeDtypeStruct((n_slots, f), BF16),
        compiler_params=_params("arbitrary", "arbitrary"),
        name="expert_ffn1",
    )(blk_expert, blk_new, xs, w1, w1, b1.reshape(n_exp, 1, two_f), b1.reshape(n_exp, 1, two_f))


def _ffn2_kernel(be_ref, state_ref, a_ref, w_ref, b_ref, o_ref, w16, *, rows, cpt):
    del be_ref
    state = state_ref[pl.program_id(1)]

    @pl.when(state == BLK_NEW_EXPERT)
    def _():
        w16[...] = w_ref[0].astype(BF16)

    @pl.when(state != BLK_UNUSED)
    def _():
        y = jnp.dot(a_ref[...], w16[...], preferred_element_type=F32) + b_ref[0]
        for col in range(cpt):
            o_ref[0, pl.ds(col, rows, stride=cpt), :] = y[:, col * LANES:(col + 1) * LANES]

    @pl.when(state == BLK_UNUSED)
    def _():
        o_ref[...] = jnp.zeros_like(o_ref)


def _ffn2(act, w2, b2, blk_expert, blk_state):
    n_slots, f = act.shape
    n_exp, _, d = w2.shape
    rows = MOE_ROWS
    tn = _tile(d, 1024)
    cpt = tn // LANES
    grid_spec = pltpu.PrefetchScalarGridSpec(
        num_scalar_prefetch=2,
        grid=(d // tn, n_slots // rows),
        in_specs=[pl.BlockSpec((rows, f), lambda j, i, be, st: (i, 0)),
                  pl.BlockSpec((1, f, tn), lambda j, i, be, st: (be[i], 0, j)),
                  pl.BlockSpec((1, 1, tn), lambda j, i, be, st: (be[i], 0, j))],
        out_specs=pl.BlockSpec((1, rows * cpt, LANES), lambda j, i, be, st: (j, i, 0)),
        scratch_shapes=[pltpu.VMEM((f, tn), BF16)],
    )
    return pl.pallas_call(
        functools.partial(_ffn2_kernel, rows=rows, cpt=cpt),
        grid_spec=grid_spec,
        out_shape=jax.ShapeDtypeStruct((d // tn, n_slots * cpt, LANES), F32),
        compiler_params=_params("arbitrary", "arbitrary"),
        name="expert_ffn2",
    )(blk_expert, blk_state, act, w2, b2.reshape(n_exp, 1, d))


def _combine_kernel(dest_ref, x_ref, tw_ref, gt_ref, gf_ref, ys_hbm, o_ref, buf, sem, *, rows, nj, cpt):
    def start(i, carry):
        src = ys_hbm.at[:, pl.ds(pl.multiple_of(dest_ref[i] * cpt, cpt), cpt), :]
        dst = buf.at[i % TOP_K, :, pl.ds(pl.multiple_of((i // TOP_K) * cpt, cpt), cpt), :]
        pltpu.make_async_copy(src, dst, sem).start()
        return carry

    lax.fori_loop(0, rows * TOP_K, start, 0, unroll=8)
    for k in range(TOP_K):
        pltpu.make_async_copy(ys_hbm.at[:, pl.ds(0, rows * cpt), :], buf.at[k], sem).wait()
    tw = tw_ref[...]
    twb = [jnp.broadcast_to(tw[:, k:k + 1], (rows, LANES)) for k in range(TOP_K)]
    ssq = jnp.zeros((rows, 1), F32)
    for j in range(nj):
        for col in range(cpt):
            lo = (j * cpt + col) * LANES
            y = buf[0, j, pl.ds(col, rows, stride=cpt), :] * twb[0]
            for k in range(1, TOP_K):
                y = y + buf[k, j, pl.ds(col, rows, stride=cpt), :] * twb[k]
            x = x_ref[:, lo:lo + LANES] + gt_ref[0][:, lo:lo + LANES] * y
            o_ref[:, lo:lo + LANES] = x
            ssq = ssq + jnp.sum(x * x, axis=-1, keepdims=True)
    d = o_ref.shape[1]
    o_ref[...] = o_ref[...] * lax.rsqrt(ssq * (1.0 / d) + EPS) * gf_ref[...]


def _combine(x1, tw, gate, normf_g, ys, dest_flat, seq):
    m, d = x1.shape
    nj = ys.shape[0]
    cpt = d // nj // LANES
    rows = _tile(seq, 128, SUBLANES)
    per_batch = seq // rows
    return pl.pallas_call(
        functools.partial(_combine_kernel, rows=rows, nj=nj, cpt=cpt),
        grid=(m // rows,),
        in_specs=[pl.BlockSpec((rows * TOP_K,), lambda i: (i,), memory_space=pltpu.SMEM),
                  pl.BlockSpec((rows, d), lambda i: (i, 0)),
                  pl.BlockSpec((rows, LANES), lambda i: (i, 0)),
                  pl.BlockSpec((1, 1, d), lambda i: (i // per_batch, 0, 0)),
                  pl.BlockSpec((1, d), lambda i: (0, 0)),
                  pl.BlockSpec(memory_space=pl.ANY)],
        out_specs=pl.BlockSpec((rows, d), lambda i: (i, 0)),
        out_shape=jax.ShapeDtypeStruct((m, d), F32),
        scratch_shapes=[pltpu.VMEM((TOP_K, nj, rows * cpt, LANES), F32), pltpu.SemaphoreType.DMA(())],
        compiler_params=_params("arbitrary"),
        name="combine",
    )(dest_flat, x1, tw, gate, normf_g.reshape(1, d), ys)


def kernel(x, c, ctx, c_ctx, w_mod, b_mod, norm1_g, w_in, conv_w, a_log, dt_bias, onorm_g, gm_ln_g, gm_ln_b,
           gm_ws, gm_bs, w_up_a, w_up_b, w_o, norm2_g, w_router, b_router, w1, b1, w2, b2, normf_g):
    depth = w_mod.shape[0]
    assert depth == 1, "single-layer block"
    bsz, seq, d = x.shape
    ctx_len = ctx.shape[1]
    n_heads = a_log.shape[-1]
    dn_w = n_heads * HEAD
    gm_w = gm_ln_g.shape[-1]
    n_exp = w_router.shape[-1]
    assert seq % GRID_W == 0 and seq % GM_CHUNK == 0 and seq % DN_CHUNK == 0 and ctx_len % DN_CHUNK == 0
    assert 4 * n_heads <= LANES and GM_CHUNK % (seq // GRID_W) == 0
    (w_mod, b_mod, norm1_g, w_in, conv_w, a_log, dt_bias, onorm_g, gm_ln_g, gm_ln_b, gm_ws, gm_bs, w_up_a, w_up_b,
     w_o, norm2_g, w_router, b_router, w1, b1, w2, b2) = (
        t[0] for t in (w_mod, b_mod, norm1_g, w_in, conv_w, a_log, dt_bias, onorm_g, gm_ln_g, gm_ln_b, gm_ws, gm_bs,
                       w_up_a, w_up_b, w_o, norm2_g, w_router, b_router, w1, b1, w2, b2))

    n_rows = -(-(bsz + 1) // SUBLANES) * SUBLANES
    cond = jnp.concatenate([c, c_ctx[None], jnp.zeros((n_rows - bsz - 1, d), F32)], axis=0)
    mod = _modulation(cond, w_mod, b_mod)
    sh1, sc1, gt1, sh2, sc2, gt2 = (mod[:, i * d:(i + 1) * d].reshape(n_rows, 1, d) for i in range(N_MOD))

    col_decay = 2 * dn_w
    col_q = col_decay + 4 * n_heads
    w_main = jnp.concatenate([w_in[:, :col_decay], w_in[:, col_q:]], axis=1).astype(BF16)
    w_dec = jnp.concatenate([w_in[:, col_decay:col_q], jnp.zeros((d, LANES - 4 * n_heads), F32)], axis=1).astype(BF16)
    col_z = 3 * dn_w
    col_gu = col_z + dn_w
    col_gv = col_gu + gm_w
    col_merge = col_gv + gm_w

    hc = _norm_modulate(ctx, norm1_g, sc1, sh1, lambda i: bsz).reshape(bsz * ctx_len, d)
    pc = _matmul(hc, w_main[:, :2 * dn_w], F32).reshape(bsz, ctx_len, 2 * dn_w)
    rawc = _matmul(hc, w_dec, F32).reshape(bsz, ctx_len, LANES)
    kvc = _short_conv(pc, conv_w, dn_w, 2)
    colsc, dmatc = _decay_prep(rawc, a_log, dt_bias, n_heads)
    zero_state = jnp.zeros((bsz, 2, n_heads, HEAD, HEAD), F32)
    _, _, ctx_state = _delta_scan(kvc, colsc, dmatc, zero_state, n_heads, False)

    h = _norm_modulate(x, norm1_g, sc1, sh1, lambda i: i).reshape(bsz * seq, d)
    p2d = _matmul(h, w_main, F32)
    p = p2d.reshape(bsz, seq, -1)
    raw = _matmul(h, w_dec, F32).reshape(bsz, seq, LANES)
    kvq = _short_conv(p, conv_w, dn_w, 3)
    cols, dmat = _decay_prep(raw, a_log, dt_bias, n_heads)
    o_f, o_b, _ = _delta_scan(kvq, cols, dmat, ctx_state, n_heads, True)
    y_a = _gated_norm(o_f, o_b, p, col_z, onorm_g).reshape(bsz * seq, dn_w)
    gvn = _gelu_layernorm(p, col_gv, gm_w, gm_ln_g, gm_ln_b)
    y_b = _spatial_gate(gvn, p, col_gu, gm_ws, gm_bs).reshape(bsz * seq, gm_w)
    merged = _merge(y_a, y_b, w_up_a.astype(BF16), w_up_b.astype(BF16), p2d, col_merge)
    x2d = x.reshape(bsz * seq, d)
    x1 = _out_proj_residual(merged, w_o.astype(BF16), x2d, gt1, seq)

    h2, ti, tw, rk, cnt = _router(x1, norm2_g, sc2, sh2, w_router, b_router, seq)
    n_tok = bsz * seq
    top_i = ti[:, :TOP_K]
    counts = cnt[0].astype(I32)
    padded = (counts + MOE_ROWS - 1) // MOE_ROWS * MOE_ROWS
    pad_end = jnp.cumsum(padded)
    pad_start = pad_end - padded
    dest = (pad_start[top_i] + rk[:, :TOP_K]).reshape(-1)
    n_slots = n_tok * TOP_K + n_exp * MOE_ROWS
    tok_flat = jnp.repeat(jnp.arange(n_tok, dtype=I32), TOP_K)
    slot_tok = jnp.zeros((n_slots,), I32).at[dest].set(tok_flat)
    blk_start = jnp.arange(n_slots // MOE_ROWS, dtype=I32) * MOE_ROWS
    blk_expert = jnp.minimum(jnp.sum((pad_end[None, :] <= blk_start[:, None]).astype(I32), axis=1), n_exp - 1)
    changed = jnp.concatenate([jnp.ones((1,), jnp.bool_), blk_expert[1:] != blk_expert[:-1]])
    blk_state = jnp.where(blk_start >= pad_end[-1], BLK_UNUSED,
                          jnp.where(changed, BLK_NEW_EXPERT, BLK_SAME_EXPERT)).astype(I32)
    xs = _dispatch(h2, slot_tok, blk_state, d)
    act = _ffn1(xs, w1, b1, blk_expert, blk_state)
    ys = _ffn2(act, w2, b2, blk_expert, blk_state)
    out = _combine(x1, tw, gt2, normf_g, ys, dest, seq)
    return out.reshape(bsz, seq, d)
```

```python
import functools

import jax
import jax.numpy as jnp
from jax import lax
from jax.experimental import pallas as pl
from jax.experimental.pallas import tpu as pltpu

F32 = jnp.float32
BF16 = jnp.bfloat16
I32 = jnp.int32
U32 = jnp.uint32
HIGHEST = lax.Precision.HIGHEST

EPS = 1e-6
N_MOD = 6
GRID_W = 64
HEAD = 128
DN_CHUNK = 64
GM_CHUNK = 128
TOP_K = 4
SWIGLU_LIMIT = 7.0
SWIGLU_ALPHA = 1.702
MOE_ROWS = 256
BLK_SAME_EXPERT, BLK_NEW_EXPERT, BLK_UNUSED = 0, 1, 2
LANES = 128
SUBLANES = 8
VMEM_LIMIT = 56 * 1024 * 1024

NT_DIMS = (((1,), (1,)), ((), ()))
TN_DIMS = (((0,), (0,)), ((), ()))


def _tile(dim, pref, align=LANES):
    t = min(pref, dim) // align * align
    while t >= align:
        if dim % t == 0:
            return t
        t -= align
    return dim


def _params(*sem):
    return pltpu.CompilerParams(dimension_semantics=sem, vmem_limit_bytes=VMEM_LIMIT)


def _silu(x):
    return x * jax.nn.sigmoid(x)


def _gelu_tanh(x):
    c = 0.7978845608028654
    return 0.5 * x * (1.0 + jnp.tanh(c * (x + 0.044715 * (x * x * x))))


def _mod_kernel(c_ref, w_ref, b_ref, o_ref):
    s = _silu(c_ref[...]).astype(BF16)
    o_ref[...] = jnp.dot(s, w_ref[...].astype(BF16), preferred_element_type=F32) + b_ref[...]


def _modulation(cond, w_mod, b_mod):
    rows, d = cond.shape
    n = w_mod.shape[1]
    tn = _tile(n, 512)
    return pl.pallas_call(
        _mod_kernel,
        grid=(n // tn,),
        in_specs=[pl.BlockSpec((rows, d), lambda j: (0, 0)),
                  pl.BlockSpec((d, tn), lambda j: (0, j)),
                  pl.BlockSpec((1, tn), lambda j: (0, j))],
        out_specs=pl.BlockSpec((rows, tn), lambda j: (0, j)),
        out_shape=jax.ShapeDtypeStruct((rows, n), F32),
        compiler_params=_params("parallel"),
        name="modulation",
    )(cond, w_mod, b_mod.reshape(1, n))


def _rms_mod(x, g, sc, sh):
    y = x * lax.rsqrt(jnp.mean(x * x, axis=-1, keepdims=True) + EPS)
    return (y * g) * (1.0 + sc) + sh


def _normmod_kernel(x_ref, g_ref, sc_ref, sh_ref, o_ref):
    o_ref[0] = _rms_mod(x_ref[0], g_ref[...], sc_ref[0], sh_ref[0]).astype(o_ref.dtype)


def _norm_modulate(x, g, sc, sh, row_of_batch):
    b, l, d = x.shape
    tl = _tile(l, 512, SUBLANES)
    return pl.pallas_call(
        _normmod_kernel,
        grid=(b, l // tl),
        in_specs=[pl.BlockSpec((1, tl, d), lambda i, j: (i, j, 0)),
                  pl.BlockSpec((1, d), lambda i, j: (0, 0)),
                  pl.BlockSpec((1, 1, d), lambda i, j: (row_of_batch(i), 0, 0)),
                  pl.BlockSpec((1, 1, d), lambda i, j: (row_of_batch(i), 0, 0))],
        out_specs=pl.BlockSpec((1, tl, d), lambda i, j: (i, j, 0)),
        out_shape=jax.ShapeDtypeStruct((b, l, d), BF16),
        compiler_params=_params("parallel", "parallel"),
        name="norm_modulate",
    )(x, g.reshape(1, d), sc, sh)


def _mm_kernel(a_ref, w_ref, o_ref):
    o_ref[...] = jnp.dot(a_ref[...], w_ref[...], preferred_element_type=F32).astype(o_ref.dtype)


def _matmul(a, w, out_dtype, tm_pref=1024, tn_pref=1024):
    m, k = a.shape
    n = w.shape[1]
    tm = _tile(m, tm_pref, SUBLANES)
    tn = _tile(n, tn_pref)
    return pl.pallas_call(
        _mm_kernel,
        grid=(m // tm, n // tn),
        in_specs=[pl.BlockSpec((tm, k), lambda i, j: (i, 0)),
                  pl.BlockSpec((k, tn), lambda i, j: (0, j))],
        out_specs=pl.BlockSpec((tm, tn), lambda i, j: (i, j)),
        out_shape=jax.ShapeDtypeStruct((m, n), out_dtype),
        compiler_params=_params("parallel", "parallel"),
        name="matmul",
    )(a, w)


def _conv_kernel(prev_ref, cur_ref, next_ref, w_ref, o_ref, *, tl, n_t, tc, k_tiles, conv_k):
    i = pl.program_id(1)
    j = pl.program_id(2)
    prev = jnp.where(i > 0, prev_ref[0], 0.0)
    nxt = jnp.where(i < n_t - 1, next_ref[0], 0.0)
    xx = jnp.concatenate([prev, cur_ref[0], nxt], axis=0)
    n = tl + 2 * SUBLANES
    acc = None
    for jj in range(conv_k):
        s = jj - conv_k // 2
        shifted = xx if s == 0 else pltpu.roll(xx, (-s) % n, 0)
        term = shifted[SUBLANES:SUBLANES + tl] * w_ref[jj:jj + 1, :]
        acc = term if acc is None else acc + term
    y = _silu(acc)
    is_k = j < k_tiles
    is_q = j >= 2 * k_tiles
    outs = []
    for hh in range(tc // HEAD):
        yh = y[:, hh * HEAD:(hh + 1) * HEAD]
        inv = lax.rsqrt(jnp.sum(yh * yh, axis=-1, keepdims=True) + EPS)
        scale = jnp.where(is_k, inv, jnp.where(is_q, inv * (HEAD ** -0.5), 1.0))
        outs.append(yh * scale)
    o_ref[0] = jnp.concatenate(outs, axis=-1)


def _short_conv(p, conv_w, width, n_sections):
    b, l, _ = p.shape
    conv_k = conv_w.shape[0]
    c = n_sections * width
    tc = _tile(width, 512)
    tl = _tile(l, 512, SUBLANES)
    n_t = l // tl
    sub = tl // SUBLANES
    kern = functools.partial(_conv_kernel, tl=tl, n_t=n_t, tc=tc, k_tiles=width // tc, conv_k=conv_k)
    return pl.pallas_call(
        kern,
        grid=(b, n_t, c // tc),
        in_specs=[
            pl.BlockSpec((1, SUBLANES, tc), lambda bi, i, j: (bi, jnp.maximum(i * sub - 1, 0), j)),
            pl.BlockSpec((1, tl, tc), lambda bi, i, j: (bi, i, j)),
            pl.BlockSpec((1, SUBLANES, tc), lambda bi, i, j: (bi, jnp.minimum((i + 1) * sub, l // SUBLANES - 1), j)),
            pl.BlockSpec((conv_k, tc), lambda bi, i, j: (0, j)),
        ],
        out_specs=pl.BlockSpec((1, tl, tc), lambda bi, i, j: (bi, i, j)),
        out_shape=jax.ShapeDtypeStruct((b, l, c), F32),
        compiler_params=_params("parallel", "parallel", "parallel"),
        name="short_conv",
    )(p, p, p, conv_w[:, :c])


def _decay_kernel(raw_ref, alog_ref, dtb_ref, cols_ref, dmat_ref, *, n_heads):
    h = n_heads
    c = DN_CHUNK
    raw = raw_ref[0]
    x = raw + dtb_ref[...]
    softplus = jnp.maximum(x, 0.0) + jnp.log1p(jnp.exp(-jnp.abs(x)))
    gs = -jnp.exp(alog_ref[...]) * softplus
    beta = jax.nn.sigmoid(raw)
    ii = lax.broadcasted_iota(I32, (c, c), 0)
    jj = lax.broadcasted_iota(I32, (c, c), 1)
    tri = [jj <= ii, jj >= ii]
    trif = [t.astype(F32) for t in tri]
    eye = (lax.broadcasted_iota(I32, (LANES, LANES), 0) == lax.broadcasted_iota(I32, (LANES, LANES), 1)).astype(F32)
    gs_t = lax.dot_general(eye, gs, NT_DIMS, precision=HIGHEST, preferred_element_type=F32)
    lane = lax.broadcasted_iota(I32, (c, LANES), 1)
    g_dir = [jnp.dot(trif[d], gs, precision=HIGHEST, preferred_element_type=F32) for d in range(2)]
    g_cum = jnp.where(lane < h, g_dir[0], g_dir[1])
    g_last = jnp.where(lane < h, g_dir[0][c - 1:c, :], g_dir[1][0:1, :])
    cols_ref[0] = jnp.concatenate([beta, jnp.exp(g_cum), jnp.exp(g_last - g_cum), jnp.exp(g_last)], axis=-1)
    for d in range(2):
        g_row = lax.dot_general(gs_t, trif[d], NT_DIMS, precision=HIGHEST, preferred_element_type=F32)
        mats = []
        for hh in range(h):
            col = d * h + hh
            diff = g_dir[d][:, col:col + 1] - g_row[col:col + 1, :]
            mats.append(jnp.where(tri[d], jnp.exp(jnp.where(tri[d], diff, 0.0)), 0.0))
        dmat_ref[0, d] = jnp.concatenate(mats, axis=-1)


def _decay_prep(raw, a_log, dt_bias, n_heads):
    b, l, _ = raw.shape
    h = n_heads
    pad = LANES - 2 * h
    alog = jnp.concatenate([a_log.reshape(1, 2 * h), jnp.zeros((1, pad), F32)], axis=-1)
    dtb = jnp.concatenate([dt_bias.reshape(1, 2 * h), jnp.zeros((1, pad), F32)], axis=-1)
    n = l // DN_CHUNK
    return pl.pallas_call(
        functools.partial(_decay_kernel, n_heads=h),
        grid=(b, n),
        in_specs=[pl.BlockSpec((1, DN_CHUNK, LANES), lambda i, j: (i, j, 0)),
                  pl.BlockSpec((1, LANES), lambda i, j: (0, 0)),
                  pl.BlockSpec((1, LANES), lambda i, j: (0, 0))],
        out_specs=[pl.BlockSpec((1, DN_CHUNK, 4 * LANES), lambda i, j: (i, j, 0)),
                   pl.BlockSpec((1, 2, DN_CHUNK, h * DN_CHUNK), lambda i, j: (i, 0, j, 0))],
        out_shape=[jax.ShapeDtypeStruct((b, l, 4 * LANES), F32),
                   jax.ShapeDtypeStruct((b, 2, l, h * DN_CHUNK), F32)],
        compiler_params=_params("parallel", "parallel"),
        name="decay_prep",
    )(raw, alog, dtb)


def _dot16(a, b, dims=None):
    a = a.astype(BF16)
    b = b.astype(BF16)
    if dims is None:
        return jnp.dot(a, b, preferred_element_type=F32)
    return lax.dot_general(a, b, dims, preferred_element_type=F32)


def _unit_triangular_inverses(mats, ii, jj):
    eye = (ii == jj).astype(F32)
    blk8 = (ii // 8) == (jj // 8)
    a0 = [jnp.where(blk8, a, 0.0) for a in mats]
    a2 = [_dot16(x, x) for x in a0]
    a4 = [_dot16(x, x) for x in a2]
    p = [eye - x for x in a0]
    p = [x + _dot16(x, y) for x, y in zip(p, a2)]
    p = [x + _dot16(x, y) for x, y in zip(p, a4)]
    for s in (8, 16, 32):
        sel = ((ii // (2 * s)) == (jj // (2 * s))) & ((ii // s) != (jj // s))
        t = [_dot16(jnp.where(sel, a, 0.0), x) for a, x in zip(mats, p)]
        p = [x - _dot16(x, y) for x, y in zip(p, t)]
    return p


def _delta_kernel(*refs, hb, n_heads, with_q, cps):
    if with_q:
        (kf_ref, vf_ref, qf_ref, kb_ref, vb_ref, qb_ref, df_ref, db_ref, cf_ref, cb_ref, s0_ref,
         of_ref, ob_ref, sfin_ref, s_ref) = refs
        q_refs = (qf_ref, qb_ref)
        o_refs = (of_ref, ob_ref)
    else:
        (kf_ref, vf_ref, kb_ref, vb_ref, df_ref, db_ref, cf_ref, cb_ref, s0_ref, sfin_ref, s_ref) = refs
    k_refs = (kf_ref, kb_ref)
    v_refs = (vf_ref, vb_ref)
    d_refs = (df_ref, db_ref)
    c_refs = (cf_ref, cb_ref)
    hg = pl.program_id(1)
    n = pl.program_id(2)
    c = DN_CHUNK

    @pl.when(n == 0)
    def _():
        s_ref[...] = s0_ref[0]

    ii = lax.broadcasted_iota(I32, (c, c), 0)
    jj = lax.broadcasted_iota(I32, (c, c), 1)
    lane = lax.broadcasted_iota(I32, (c, LANES), 1)

    def column(x, idx):
        return jnp.sum(jnp.where(lane == idx, x, 0.0), axis=-1, keepdims=True)

    chains = [(d, hh) for d in range(2) for hh in range(hb)]
    items = [(d, hh, ci) for ci in range(cps) for d, hh in chains]

    def rows_of(d, ci):
        r0 = (ci if d == 0 else cps - 1 - ci) * c
        return slice(r0, r0 + c)

    beta, eg, ekend, eglast, k, v, dm = [], [], [], [], [], [], []
    for d, hh, ci in items:
        head = hg * hb + hh
        cols = c_refs[d][0][rows_of(d, ci), :]
        beta.append(column(cols[:, 0:LANES], (2 + d) * n_heads + head))
        eg.append(column(cols[:, LANES:2 * LANES], d * n_heads + head))
        ekend.append(column(cols[:, 2 * LANES:3 * LANES], d * n_heads + head))
        eglast.append(column(cols[:, 3 * LANES:4 * LANES], d * n_heads + head)[0:1, :])
        k.append(k_refs[d][0][rows_of(d, ci), hh * HEAD:(hh + 1) * HEAD])
        v.append(v_refs[d][0][rows_of(d, ci), hh * HEAD:(hh + 1) * HEAD])
        dm.append(d_refs[d][0, 0][rows_of(d, ci), hh * c:(hh + 1) * c])
    strict = [(jj < ii) if d == 0 else (jj > ii) for d, _, _ in items]
    kb = [x * y for x, y in zip(k, beta)]
    kk = [_dot16(x, y, NT_DIMS) for x, y in zip(kb, k)]
    a = [jnp.where(m, x * y, 0.0) for m, x, y in zip(strict, kk, dm)]
    t = _unit_triangular_inverses(a, ii, jj)
    u = [_dot16(x, y * z) for x, y, z in zip(t, v, beta)]
    w = [_dot16(x, y * z) for x, y, z in zip(t, kb, eg)]
    ke = [x * y for x, y in zip(k, ekend)]
    if with_q:
        q = [q_refs[d][0][rows_of(d, ci), hh * HEAD:(hh + 1) * HEAD] for d, hh, ci in items]
        qk = [_dot16(x, y, NT_DIMS) * z for x, y, z in zip(q, k, dm)]
        qd = [x * y for x, y in zip(q, eg)]
        o = {}
    s = [s_ref[d, hh] for d, hh in chains]
    for ci in range(cps):
        idx = [ci * len(chains) + n_c for n_c in range(len(chains))]
        v_new = [u[i] - _dot16(w[i], x) for i, x in zip(idx, s)]
        if with_q:
            for i, x, vn, (d, hh) in zip(idx, s, v_new, chains):
                o[(d, hh, ci)] = _dot16(qd[i], x) + _dot16(qk[i], vn)
        upd = [_dot16(ke[i], vn, TN_DIMS) for i, vn in zip(idx, v_new)]
        s = [x * eglast[i] + z for i, x, z in zip(idx, s, upd)]
    for (d, hh), x in zip(chains, s):
        s_ref[d, hh] = x
    if with_q:
        for d in range(2):
            order = range(cps) if d == 0 else range(cps - 1, -1, -1)
            o_refs[d][0] = jnp.concatenate(
                [jnp.concatenate([o[(d, hh, ci)] for hh in range(hb)], axis=-1) for ci in order], axis=0)

    @pl.when(n == pl.num_programs(2) - 1)
    def _():
        sfin_ref[0] = s_ref[...]


def _delta_scan(kvq, cols, dmat, s0, n_heads, with_q):
    b, l, _ = kvq.shape
    h = n_heads
    hb = 4 if h % 4 == 0 else (2 if h % 2 == 0 else 1)
    n_chunks = l // DN_CHUNK
    cps = 4 if n_chunks % 4 == 0 else (2 if n_chunks % 2 == 0 else 1)
    c = cps * DN_CHUNK
    n = l // c
    wblk = h // hb

    def sec(section, rev):
        def imap(bi, g, j):
            return (bi, (n - 1 - j) if rev else j, section * wblk + g)
        return pl.BlockSpec((1, c, hb * HEAD), imap)

    def dspec(d):
        return pl.BlockSpec((1, 1, c, hb * DN_CHUNK), lambda bi, g, j: (bi, d, (n - 1 - j) if d else j, g))

    def cspec(d):
        return pl.BlockSpec((1, c, 4 * LANES), lambda bi, g, j: (bi, (n - 1 - j) if d else j, 0))

    state_spec = pl.BlockSpec((1, 2, hb, HEAD, HEAD), lambda bi, g, j: (bi, 0, g, 0, 0))
    n_sec = 3 if with_q else 2
    in_specs = [sec(s, False) for s in range(n_sec)] + [sec(s, True) for s in range(n_sec)]
    in_specs += [dspec(0), dspec(1), cspec(0), cspec(1), state_spec]
    args = [kvq] * (2 * n_sec) + [dmat, dmat, cols, cols, s0]
    out_specs = [state_spec]
    out_shape = [jax.ShapeDtypeStruct((b, 2, h, HEAD, HEAD), F32)]
    if with_q:
        ospec = [pl.BlockSpec((1, c, hb * HEAD), lambda bi, g, j: (bi, j, g)),
                 pl.BlockSpec((1, c, hb * HEAD), lambda bi, g, j: (bi, n - 1 - j, g))]
        out_specs = ospec + out_specs
        out_shape = [jax.ShapeDtypeStruct((b, l, h * HEAD), F32)] * 2 + out_shape
    res = pl.pallas_call(
        functools.partial(_delta_kernel, hb=hb, n_heads=h, with_q=with_q, cps=cps),
        grid=(b, wblk, n),
        in_specs=in_specs,
        out_specs=out_specs,
        out_shape=out_shape,
        scratch_shapes=[pltpu.VMEM((2, hb, HEAD, HEAD), F32)],
        compiler_params=_params("parallel", "parallel", "arbitrary"),
        name="delta_scan_q" if with_q else "delta_scan_state",
    )(*args)
    if with_q:
        return res[0], res[1], res[2]
    return None, None, res[0]


def _gated_norm_kernel(of_ref, ob_ref, z_ref, g_ref, o_ref):
    o = of_ref[0] + ob_ref[0]
    z = z_ref[0]
    outs = []
    for hh in range(o.shape[-1] // HEAD):
        oh = o[:, hh * HEAD:(hh + 1) * HEAD]
        y = oh * lax.rsqrt(jnp.mean(oh * oh, axis=-1, keepdims=True) + EPS) * g_ref[...]
        outs.append(y * _silu(z[:, hh * HEAD:(hh + 1) * HEAD]))
    o_ref[0] = jnp.concatenate(outs, axis=-1).astype(o_ref.dtype)


def _gated_norm(o_f, o_b, p, z_col, onorm_g):
    b, l, w = o_f.shape
    tl = _tile(l, 512, SUBLANES)
    tc = _tile(w, 512)
    zb = z_col // tc
    return pl.pallas_call(
        _gated_norm_kernel,
        grid=(b, l // tl, w // tc),
        in_specs=[pl.BlockSpec((1, tl, tc), lambda i, j, k: (i, j, k)),
                  pl.BlockSpec((1, tl, tc), lambda i, j, k: (i, j, k)),
                  pl.BlockSpec((1, tl, tc), lambda i, j, k: (i, j, zb + k)),
                  pl.BlockSpec((1, HEAD), lambda i, j, k: (0, 0))],
        out_specs=pl.BlockSpec((1, tl, tc), lambda i, j, k: (i, j, k)),
        out_shape=jax.ShapeDtypeStruct((b, l, w), BF16),
        compiler_params=_params("parallel", "parallel", "parallel"),
        name="gated_norm",
    )(o_f, o_b, p, onorm_g.reshape(1, HEAD))


def _gelu_ln_kernel(x_ref, g_ref, b_ref, o_ref):
    x = _gelu_tanh(x_ref[0])
    mu = jnp.mean(x, axis=-1, keepdims=True)
    xc = x - mu
    var = jnp.mean(xc * xc, axis=-1, keepdims=True)
    o_ref[0] = (xc * lax.rsqrt(var + EPS)) * g_ref[...] + b_ref[...]


def _gelu_layernorm(p, col, width, g, bias):
    b, l, _ = p.shape
    tl = _tile(l, 256, SUBLANES)
    cb = col // width
    return pl.pallas_call(
        _gelu_ln_kernel,
        grid=(b, l // tl),
        in_specs=[pl.BlockSpec((1, tl, width), lambda i, j: (i, j, cb)),
                  pl.BlockSpec((1, width), lambda i, j: (0, 0)),
                  pl.BlockSpec((1, width), lambda i, j: (0, 0))],
        out_specs=pl.BlockSpec((1, tl, width), lambda i, j: (i, j, 0)),
        out_shape=jax.ShapeDtypeStruct((b, l, width), F32),
        compiler_params=_params("parallel", "parallel"),
        name="gelu_layernorm",
    )(p, g.reshape(1, width), bias.reshape(1, width))


def _spatial_kernel(gv_ref, gu_ref, ws_ref, bs_ref, o_ref, *, seq, row_groups, grid_w):
    g = pl.program_id(1)
    w = ws_ref[0].astype(BF16)
    bias = bs_ref[0]
    n_chunks = seq // GM_CHUNK
    rows = seq // grid_w
    cols_per_chunk = GM_CHUNK // rows

    @pl.when(g < row_groups)
    def _():
        for n in range(n_chunks):
            sl = pl.ds(n * GM_CHUNK, GM_CHUNK)
            s = jnp.dot(w, gv_ref[0, sl, :].astype(BF16), preferred_element_type=F32) + bias
            o_ref[0, sl, :] = (_gelu_tanh(gu_ref[0, sl, :]) * s).astype(o_ref.dtype)

    @pl.when(g >= row_groups)
    def _():
        for n in range(n_chunks):
            sls = [pl.ds(n * cols_per_chunk + cc, rows, stride=grid_w) for cc in range(cols_per_chunk)]
            v = jnp.concatenate([gv_ref[0, sl, :] for sl in sls], axis=0)
            s = jnp.dot(w, v.astype(BF16), preferred_element_type=F32) + bias
            for cc, sl in enumerate(sls):
                o_ref[0, sl, :] = (_gelu_tanh(gu_ref[0, sl, :]) * s[cc * rows:(cc + 1) * rows]).astype(o_ref.dtype)


def _spatial_gate(gvn, p, gu_col, gm_ws, gm_bs):
    b, l, width = gvn.shape
    groups = gm_ws.shape[0]
    gub = gu_col // HEAD
    kern = functools.partial(_spatial_kernel, seq=l, row_groups=groups // 2, grid_w=GRID_W)
    return pl.pallas_call(
        kern,
        grid=(b, groups),
        in_specs=[pl.BlockSpec((1, l, HEAD), lambda i, g: (i, 0, g)),
                  pl.BlockSpec((1, l, HEAD), lambda i, g: (i, 0, gub + g)),
                  pl.BlockSpec((1, GM_CHUNK, GM_CHUNK), lambda i, g: (g, 0, 0)),
                  pl.BlockSpec((1, GM_CHUNK, 1), lambda i, g: (g, 0, 0))],
        out_specs=pl.BlockSpec((1, l, HEAD), lambda i, g: (i, 0, g)),
        out_shape=jax.ShapeDtypeStruct((b, l, width), F32),
        compiler_params=_params("parallel", "parallel"),
        name="spatial_gate",
    )(gvn, p, gm_ws, gm_bs.reshape(groups, GM_CHUNK, 1))


def _merge_kernel(ya_ref, yb_ref, wa_ref, wb_ref, ga_ref, gb_ref, o_ref):
    pa = jnp.dot(ya_ref[...], wa_ref[...], preferred_element_type=F32)
    pb = jnp.dot(yb_ref[...].astype(BF16), wb_ref[...], preferred_element_type=F32)
    o_ref[...] = (jax.nn.sigmoid(ga_ref[...]) * pa + jax.nn.sigmoid(gb_ref[...]) * pb).astype(o_ref.dtype)


def _merge(ya, yb, wa, wb, p2d, merge_col):
    m, ka = ya.shape
    kb = yb.shape[1]
    d = wa.shape[1]
    tm = _tile(m, 512, SUBLANES)
    tn = _tile(d, 1024)
    ca = merge_col // tn
    cb = (merge_col + d) // tn
    return pl.pallas_call(
        _merge_kernel,
        grid=(m // tm, d // tn),
        in_specs=[pl.BlockSpec((tm, ka), lambda i, j: (i, 0)),
                  pl.BlockSpec((tm, kb), lambda i, j: (i, 0)),
                  pl.BlockSpec((ka, tn), lambda i, j: (0, j)),
                  pl.BlockSpec((kb, tn), lambda i, j: (0, j)),
                  pl.BlockSpec((tm, tn), lambda i, j: (i, ca + j)),
                  pl.BlockSpec((tm, tn), lambda i, j: (i, cb + j))],
        out_specs=pl.BlockSpec((tm, tn), lambda i, j: (i, j)),
        out_shape=jax.ShapeDtypeStruct((m, d), BF16),
        compiler_params=_params("parallel", "parallel"),
        name="merge",
    )(ya, yb, wa, wb, p2d, p2d)


def _oproj_kernel(a_ref, w_ref, x_ref, gt_ref, o_ref):
    y = jnp.dot(a_ref[...], w_ref[...], preferred_element_type=F32)
    o_ref[...] = x_ref[...] + gt_ref[0] * y


def _out_proj_residual(a, w, x2d, gate, seq):
    m, k = a.shape
    d = w.shape[1]
    tm = _tile(seq, 1024, SUBLANES)
    tn = _tile(d, 1024)
    per_batch = seq // tm
    return pl.pallas_call(
        _oproj_kernel,
        grid=(m // tm, d // tn),
        in_specs=[pl.BlockSpec((tm, k), lambda i, j: (i, 0)),
                  pl.BlockSpec((k, tn), lambda i, j: (0, j)),
                  pl.BlockSpec((tm, tn), lambda i, j: (i, j)),
                  pl.BlockSpec((1, 1, tn), lambda i, j: (i // per_batch, 0, j))],
        out_specs=pl.BlockSpec((tm, tn), lambda i, j: (i, j)),
        out_shape=jax.ShapeDtypeStruct((m, d), F32),
        compiler_params=_params("parallel", "parallel"),
        name="out_proj",
    )(a, w, x2d, gate)


def _router_kernel(x_ref, g_ref, sc_ref, sh_ref, wr_ref, br_ref, h_ref, ti_ref, tw_ref, rk_ref, cnt_ref, carry,
                   *, n_exp, tm):
    i = pl.program_id(0)

    @pl.when(i == 0)
    def _():
        carry[...] = jnp.zeros_like(carry)

    h2 = _rms_mod(x_ref[...], g_ref[...], sc_ref[0], sh_ref[0])
    h_hi = h2.astype(BF16)
    h_lo = (h2 - h_hi.astype(F32)).astype(BF16)
    half = h2.shape[1] // 2
    bits = lax.bitcast_convert_type(h_hi.astype(F32), U32)
    h_ref[...] = bits[:, :half] | (bits[:, half:] >> 16)
    wr = wr_ref[...]
    w_hi = wr.astype(BF16)
    w_lo = (wr - w_hi.astype(F32)).astype(BF16)
    logits = (jnp.dot(h_hi, w_hi, preferred_element_type=F32) + jnp.dot(h_hi, w_lo, preferred_element_type=F32)
              + jnp.dot(h_lo, w_hi, preferred_element_type=F32)) + br_ref[...]
    lane = lax.broadcasted_iota(I32, (tm, n_exp), 1)
    work = logits
    member = jnp.zeros((tm, n_exp), F32)
    vals, ids = [], []
    for _ in range(TOP_K):
        mx = jnp.max(work, axis=-1, keepdims=True)
        idx = jnp.min(jnp.where(work == mx, lane, n_exp), axis=-1, keepdims=True)
        hit = lane == idx
        vals.append(mx)
        ids.append(idx)
        work = jnp.where(hit, -jnp.inf, work)
        member = member + hit.astype(F32)
    exps = [jnp.exp(v - vals[0]) for v in vals]
    den = exps[0]
    for e in exps[1:]:
        den = den + e
    rr = lax.broadcasted_iota(I32, (tm, tm), 0)
    cc = lax.broadcasted_iota(I32, (tm, tm), 1)
    below = (cc < rr).astype(BF16)
    rank_all = jnp.dot(below, member.astype(BF16), preferred_element_type=F32) + carry[...]
    out_lane = lax.broadcasted_iota(I32, (tm, LANES), 1)
    ti = jnp.zeros((tm, LANES), I32)
    tw = jnp.zeros((tm, LANES), F32)
    rk = jnp.zeros((tm, LANES), F32)
    for k in range(TOP_K):
        rank_k = jnp.sum(jnp.where(lane == ids[k], rank_all, 0.0), axis=-1, keepdims=True)
        ti = jnp.where(out_lane == k, ids[k], ti)
        tw = jnp.where(out_lane == k, exps[k] / den, tw)
        rk = jnp.where(out_lane == k, rank_k, rk)
    ti_ref[...] = ti
    tw_ref[...] = tw
    rk_ref[...] = rk.astype(I32)
    carry[...] = carry[...] + jnp.sum(member, axis=0, keepdims=True)
    cnt_ref[...] = carry[...]


def _router(x2d, g, sc, sh, w_router, b_router, seq):
    m, d = x2d.shape
    n_exp = w_router.shape[1]
    tm = _tile(seq, 256, SUBLANES)
    per_batch = seq // tm
    kern = functools.partial(_router_kernel, n_exp=n_exp, tm=tm)
    row = lambda i: (i, 0)
    return pl.pallas_call(
        kern,
        grid=(m // tm,),
        in_specs=[pl.BlockSpec((tm, d), row),
                  pl.BlockSpec((1, d), lambda i: (0, 0)),
                  pl.BlockSpec((1, 1, d), lambda i: (i // per_batch, 0, 0)),
                  pl.BlockSpec((1, 1, d), lambda i: (i // per_batch, 0, 0)),
                  pl.BlockSpec((d, n_exp), lambda i: (0, 0)),
                  pl.BlockSpec((1, n_exp), lambda i: (0, 0))],
        out_specs=[pl.BlockSpec((tm, d // 2), row),
                   pl.BlockSpec((tm, LANES), row),
                   pl.BlockSpec((tm, LANES), row),
                   pl.BlockSpec((tm, LANES), row),
                   pl.BlockSpec((1, n_exp), lambda i: (0, 0))],
        out_shape=[jax.ShapeDtypeStruct((m, d // 2), U32),
                   jax.ShapeDtypeStruct((m, LANES), I32),
                   jax.ShapeDtypeStruct((m, LANES), F32),
                   jax.ShapeDtypeStruct((m, LANES), I32),
                   jax.ShapeDtypeStruct((1, n_exp), F32)],
        scratch_shapes=[pltpu.VMEM((1, n_exp), F32)],
        compiler_params=_params("arbitrary"),
        name="router",
    )(x2d, g.reshape(1, d), sc, sh, w_router, b_router.reshape(1, n_exp))


def _dispatch_kernel(state_ref, tok_ref, h_hbm, xs_ref, buf, sem, *, rows):
    live = state_ref[pl.program_id(0)] != BLK_UNUSED

    @pl.when(live)
    def _():
        def start(i, carry):
            pltpu.make_async_copy(h_hbm.at[pl.ds(tok_ref[i], 1)], buf.at[pl.ds(i, 1)], sem).start()
            return carry

        lax.fori_loop(0, rows, start, 0, unroll=8)
        pltpu.make_async_copy(h_hbm.at[pl.ds(0, rows)], buf, sem).wait()
        bits = buf[...]
        half = bits.shape[1]
        hi = lax.bitcast_convert_type(bits & jnp.uint32(0xFFFF0000), F32)
        lo = lax.bitcast_convert_type(bits << 16, F32)
        xs_ref[:, :half] = hi.astype(xs_ref.dtype)
        xs_ref[:, half:] = lo.astype(xs_ref.dtype)

    @pl.when(jnp.logical_not(live))
    def _():
        xs_ref[...] = jnp.zeros_like(xs_ref)


def _dispatch(h_packed, slot_tok, blk_state):
    n_slots = slot_tok.shape[0]
    half = h_packed.shape[1]
    rows = MOE_ROWS
    grid_spec = pltpu.PrefetchScalarGridSpec(
        num_scalar_prefetch=1,
        grid=(n_slots // rows,),
        in_specs=[pl.BlockSpec((rows,), lambda i, st: (i,), memory_space=pltpu.SMEM),
                  pl.BlockSpec(memory_space=pl.ANY)],
        out_specs=pl.BlockSpec((rows, 2 * half), lambda i, st: (i, 0)),
        scratch_shapes=[pltpu.VMEM((rows, half), U32), pltpu.SemaphoreType.DMA(())],
    )
    return pl.pallas_call(
        functools.partial(_dispatch_kernel, rows=rows),
        grid_spec=grid_spec,
        out_shape=jax.ShapeDtypeStruct((n_slots, 2 * half), BF16),
        compiler_params=_params("arbitrary"),
        name="dispatch",
    )(blk_state, slot_tok, h_packed)


def _ffn1_kernel(be_ref, state_ref, xs_ref, wg_ref, wl_ref, bg_ref, bl_ref, o_ref, wg16, wl16):
    del be_ref
    state = state_ref[pl.program_id(1)]

    @pl.when(state == BLK_NEW_EXPERT)
    def _():
        wg16[...] = wg_ref[0].astype(BF16)
        wl16[...] = wl_ref[0].astype(BF16)

    @pl.when(state != BLK_UNUSED)
    def _():
        x = xs_ref[...]
        glu = jnp.dot(x, wg16[...], preferred_element_type=F32) + bg_ref[0]
        lin = jnp.dot(x, wl16[...], preferred_element_type=F32) + bl_ref[0]
        glu = jnp.minimum(glu, SWIGLU_LIMIT)
        lin = jnp.clip(lin, -SWIGLU_LIMIT, SWIGLU_LIMIT)
        o_ref[...] = (glu * jax.nn.sigmoid(SWIGLU_ALPHA * glu) * (lin + 1.0)).astype(o_ref.dtype)

    @pl.when(state == BLK_UNUSED)
    def _():
        o_ref[...] = jnp.zeros_like(o_ref)


def _ffn1(xs, w1, b1, blk_expert, blk_new):
    n_slots, d = xs.shape
    n_exp, _, two_f = w1.shape
    f = two_f // 2
    rows = MOE_ROWS
    tn = _tile(f, 512)
    nj = f // tn
    grid_spec = pltpu.PrefetchScalarGridSpec(
        num_scalar_prefetch=2,
        grid=(nj, n_slots // rows),
        in_specs=[pl.BlockSpec((rows, d), lambda j, i, be, nw: (i, 0)),
                  pl.BlockSpec((1, d, tn), lambda j, i, be, nw: (be[i], 0, j)),
                  pl.BlockSpec((1, d, tn), lambda j, i, be, nw: (be[i], 0, nj + j)),
                  pl.BlockSpec((1, 1, tn), lambda j, i, be, nw: (be[i], 0, j)),
                  pl.BlockSpec((1, 1, tn), lambda j, i, be, nw: (be[i], 0, nj + j))],
        out_specs=pl.BlockSpec((rows, tn), lambda j, i, be, nw: (i, j)),
        scratch_shapes=[pltpu.VMEM((d, tn), BF16), pltpu.VMEM((d, tn), BF16)],
    )
    return pl.pallas_call(
        _ffn1_kernel,
        grid_spec=grid_spec,
        out_shape=jax.ShapeDtypeStruct((n_slots, f), BF16),
        compiler_params=_params("arbitrary", "arbitrary"),
        name="expert_ffn1",
    )(blk_expert, blk_new, xs, w1, w1, b1.reshape(n_exp, 1, two_f), b1.reshape(n_exp, 1, two_f))


def _ffn2_tile(d):
    return _tile(d, 2048)


def _ffn2_kernel(be_ref, state_ref, a_ref, w_ref, b_ref, o_ref, w16, *, rows, cpt):
    del be_ref
    state = state_ref[pl.program_id(1)]

    @pl.when(state == BLK_NEW_EXPERT)
    def _():
        w16[...] = w_ref[0].astype(BF16)

    @pl.when(state != BLK_UNUSED)
    def _():
        y = jnp.dot(a_ref[...], w16[...], preferred_element_type=F32) + b_ref[0]
        for col in range(cpt):
            o_ref[pl.ds(col, rows, stride=cpt), :] = y[:, col * LANES:(col + 1) * LANES]

    @pl.when(state == BLK_UNUSED)
    def _():
        o_ref[...] = jnp.zeros_like(o_ref)


def _ffn2(act, w2, b2, blk_expert, blk_state):
    n_slots, f = act.shape
    n_exp, _, d = w2.shape
    rows = MOE_ROWS
    n_blk = n_slots // rows
    tn = _ffn2_tile(d)
    cpt = tn // LANES
    grid_spec = pltpu.PrefetchScalarGridSpec(
        num_scalar_prefetch=2,
        grid=(d // tn, n_slots // rows),
        in_specs=[pl.BlockSpec((rows, f), lambda j, i, be, st: (i, 0)),
                  pl.BlockSpec((1, f, tn), lambda j, i, be, st: (be[i], 0, j)),
                  pl.BlockSpec((1, 1, tn), lambda j, i, be, st: (be[i], 0, j))],
        out_specs=pl.BlockSpec((rows * cpt, LANES), lambda j, i, be, st: (j * n_blk + i, 0)),
        scratch_shapes=[pltpu.VMEM((f, tn), BF16)],
    )
    return pl.pallas_call(
        functools.partial(_ffn2_kernel, rows=rows, cpt=cpt),
        grid_spec=grid_spec,
        out_shape=jax.ShapeDtypeStruct((d // tn * n_slots * cpt, LANES), F32),
        compiler_params=_params("arbitrary", "arbitrary"),
        name="expert_ffn2",
    )(blk_expert, blk_state, act, w2, b2.reshape(n_exp, 1, d))


def _combine_kernel(dest_ref, x_ref, tw_ref, gt_ref, gf_ref, ys_hbm, o_ref, buf, ysum, sem, *, rows, nj, cpt):
    slab_rows = ys_hbm.shape[0] // nj

    def start(t, carry):
        dst_row = pl.ds(pl.multiple_of(t * cpt, cpt), cpt)
        for k in range(TOP_K):
            row = pl.multiple_of(dest_ref[t * TOP_K + k] * cpt, cpt)
            for j in range(nj):
                pltpu.make_async_copy(ys_hbm.at[pl.ds(j * slab_rows + row, cpt)], buf.at[k, j, dst_row], sem).start()
        return carry

    lax.fori_loop(0, rows, start, 0, unroll=2)
    for k in range(TOP_K):
        for j in range(nj):
            pltpu.make_async_copy(ys_hbm.at[pl.ds(0, rows * cpt)], buf.at[k, j], sem).wait()
    tw = tw_ref[...]
    twb = [jnp.broadcast_to(tw[:, k:k + 1], (rows * cpt, LANES)) for k in range(TOP_K)]
    for j in range(nj):
        acc = buf[0, j] * twb[0]
        for k in range(1, TOP_K):
            acc = acc + buf[k, j] * twb[k]
        ysum[j] = acc
    ssq = jnp.zeros((rows, 1), F32)
    for j in range(nj):
        for col in range(cpt):
            lo = (j * cpt + col) * LANES
            y = ysum[j, pl.ds(col, rows, stride=cpt), :]
            x = x_ref[:, lo:lo + LANES] + gt_ref[0][:, lo:lo + LANES] * y
            o_ref[:, lo:lo + LANES] = x
            ssq = ssq + jnp.sum(x * x, axis=-1, keepdims=True)
    d = o_ref.shape[1]
    o_ref[...] = o_ref[...] * lax.rsqrt(ssq * (1.0 / d) + EPS) * gf_ref[...]


def _combine(x1, tw, gate, normf_g, ys, dest_flat, seq):
    m, d = x1.shape
    cpt = _ffn2_tile(d) // LANES
    nj = d // (cpt * LANES)
    rows = _tile(seq, 128, SUBLANES)
    per_batch = seq // rows
    tw_rows = jnp.repeat(tw[:, :TOP_K], cpt, axis=0)
    return pl.pallas_call(
        functools.partial(_combine_kernel, rows=rows, nj=nj, cpt=cpt),
        grid=(m // rows,),
        in_specs=[pl.BlockSpec((rows * TOP_K,), lambda i: (i,), memory_space=pltpu.SMEM),
                  pl.BlockSpec((rows, d), lambda i: (i, 0)),
                  pl.BlockSpec((rows * cpt, TOP_K), lambda i: (i, 0)),
                  pl.BlockSpec((1, 1, d), lambda i: (i // per_batch, 0, 0)),
                  pl.BlockSpec((1, d), lambda i: (0, 0)),
                  pl.BlockSpec(memory_space=pl.ANY)],
        out_specs=pl.BlockSpec((rows, d), lambda i: (i, 0)),
        out_shape=jax.ShapeDtypeStruct((m, d), F32),
        scratch_shapes=[pltpu.VMEM((TOP_K, nj, rows * cpt, LANES), F32),
                        pltpu.VMEM((nj, rows * cpt, LANES), F32),
                        pltpu.SemaphoreType.DMA(())],
        compiler_params=_params("arbitrary"),
        name="combine",
    )(dest_flat, x1, tw_rows, gate, normf_g.reshape(1, d), ys)


def kernel(x, c, ctx, c_ctx, w_mod, b_mod, norm1_g, w_in, conv_w, a_log, dt_bias, onorm_g, gm_ln_g, gm_ln_b,
           gm_ws, gm_bs, w_up_a, w_up_b, w_o, norm2_g, w_router, b_router, w1, b1, w2, b2, normf_g):
    depth = w_mod.shape[0]
    assert depth == 1, "single-layer block"
    bsz, seq, d = x.shape
    ctx_len = ctx.shape[1]
    n_heads = a_log.shape[-1]
    dn_w = n_heads * HEAD
    gm_w = gm_ln_g.shape[-1]
    n_exp = w_router.shape[-1]
    assert seq % GRID_W == 0 and seq % GM_CHUNK == 0 and seq % DN_CHUNK == 0 and ctx_len % DN_CHUNK == 0
    assert 4 * n_heads <= LANES and GM_CHUNK % (seq // GRID_W) == 0
    (w_mod, b_mod, norm1_g, w_in, conv_w, a_log, dt_bias, onorm_g, gm_ln_g, gm_ln_b, gm_ws, gm_bs, w_up_a, w_up_b,
     w_o, norm2_g, w_router, b_router, w1, b1, w2, b2) = (
        t[0] for t in (w_mod, b_mod, norm1_g, w_in, conv_w, a_log, dt_bias, onorm_g, gm_ln_g, gm_ln_b, gm_ws, gm_bs,
                       w_up_a, w_up_b, w_o, norm2_g, w_router, b_router, w1, b1, w2, b2))

    n_rows = -(-(bsz + 1) // SUBLANES) * SUBLANES
    cond = jnp.concatenate([c, c_ctx[None], jnp.zeros((n_rows - bsz - 1, d), F32)], axis=0)
    mod = _modulation(cond, w_mod, b_mod)
    sh1, sc1, gt1, sh2, sc2, gt2 = (mod[:, i * d:(i + 1) * d].reshape(n_rows, 1, d) for i in range(N_MOD))

    col_decay = 2 * dn_w
    col_q = col_decay + 4 * n_heads
    w_main = jnp.concatenate([w_in[:, :col_decay], w_in[:, col_q:]], axis=1).astype(BF16)
    w_dec = jnp.concatenate([w_in[:, col_decay:col_q], jnp.zeros((d, LANES - 4 * n_heads), F32)], axis=1).astype(BF16)
    col_z = 3 * dn_w
    col_gu = col_z + dn_w
    col_gv = col_gu + gm_w
    col_merge = col_gv + gm_w

    hc = _norm_modulate(ctx, norm1_g, sc1, sh1, lambda i: bsz).reshape(bsz * ctx_len, d)
    pc = _matmul(hc, w_main[:, :2 * dn_w], F32).reshape(bsz, ctx_len, 2 * dn_w)
    rawc = _matmul(hc, w_dec, F32).reshape(bsz, ctx_len, LANES)
    kvc = _short_conv(pc, conv_w, dn_w, 2)
    colsc, dmatc = _decay_prep(rawc, a_log, dt_bias, n_heads)
    zero_state = jnp.zeros((bsz, 2, n_heads, HEAD, HEAD), F32)
    _, _, ctx_state = _delta_scan(kvc, colsc, dmatc, zero_state, n_heads, False)

    h = _norm_modulate(x, norm1_g, sc1, sh1, lambda i: i).reshape(bsz * seq, d)
    p2d = _matmul(h, w_main, F32)
    p = p2d.reshape(bsz, seq, -1)
    raw = _matmul(h, w_dec, F32).reshape(bsz, seq, LANES)
    kvq = _short_conv(p, conv_w, dn_w, 3)
    cols, dmat = _decay_prep(raw, a_log, dt_bias, n_heads)
    o_f, o_b, _ = _delta_scan(kvq, cols, dmat, ctx_state, n_heads, True)
    y_a = _gated_norm(o_f, o_b, p, col_z, onorm_g).reshape(bsz * seq, dn_w)
    gvn = _gelu_layernorm(p, col_gv, gm_w, gm_ln_g, gm_ln_b)
    y_b = _spatial_gate(gvn, p, col_gu, gm_ws, gm_bs).reshape(bsz * seq, gm_w)
    merged = _merge(y_a, y_b, w_up_a.astype(BF16), w_up_b.astype(BF16), p2d, col_merge)
    x2d = x.reshape(bsz * seq, d)
    x1 = _out_proj_residual(merged, w_o.astype(BF16), x2d, gt1, seq)

    h2, ti, tw, rk, cnt = _router(x1, norm2_g, sc2, sh2, w_router, b_router, seq)
    n_tok = bsz * seq
    top_i = ti[:, :TOP_K]
    counts = cnt[0].astype(I32)
    padded = (counts + MOE_ROWS - 1) // MOE_ROWS * MOE_ROWS
    pad_end = jnp.cumsum(padded)
    pad_start = pad_end - padded
    dest = (pad_start[top_i] + rk[:, :TOP_K]).reshape(-1)
    n_slots = n_tok * TOP_K + n_exp * MOE_ROWS
    tok_flat = jnp.repeat(jnp.arange(n_tok, dtype=I32), TOP_K)
    slot_tok = jnp.zeros((n_slots,), I32).at[dest].set(tok_flat)
    blk_start = jnp.arange(n_slots // MOE_ROWS, dtype=I32) * MOE_ROWS
    blk_expert = jnp.minimum(jnp.sum((pad_end[None, :] <= blk_start[:, None]).astype(I32), axis=1), n_exp - 1)
    changed = jnp.concatenate([jnp.ones((1,), jnp.bool_), blk_expert[1:] != blk_expert[:-1]])
    blk_state = jnp.where(blk_start >= pad_end[-1], BLK_UNUSED,
                          jnp.where(changed, BLK_NEW_EXPERT, BLK_SAME_EXPERT)).astype(I32)
    xs = _dispatch(h2, slot_tok, blk_state)
    act = _ffn1(xs, w1, b1, blk_expert, blk_state)
    ys = _ffn2(act, w2, b2, blk_expert, blk_state)
    out = _combine(x1, tw, gt2, normf_g, ys, dest, seq)
    return out.reshape(bsz, seq, d)
```

```python
import functools

import jax
import jax.numpy as jnp
from jax import lax
from jax.experimental import pallas as pl
from jax.experimental.pallas import tpu as pltpu

F32 = jnp.float32
BF16 = jnp.bfloat16
I32 = jnp.int32
U32 = jnp.uint32
HIGHEST = lax.Precision.HIGHEST

EPS = 1e-6
N_MOD = 6
GRID_W = 64
HEAD = 128
DN_CHUNK = 64
GM_CHUNK = 128
TOP_K = 4
SWIGLU_LIMIT = 7.0
SWIGLU_ALPHA = 1.702
MOE_ROWS = 256
BLK_SAME_EXPERT, BLK_NEW_EXPERT, BLK_UNUSED = 0, 1, 2
LANES = 128
SUBLANES = 8
VMEM_LIMIT = 56 * 1024 * 1024

NT_DIMS = (((1,), (1,)), ((), ()))
TN_DIMS = (((0,), (0,)), ((), ()))


def _tile(dim, pref, align=LANES):
    t = min(pref, dim) // align * align
    while t >= align:
        if dim % t == 0:
            return t
        t -= align
    return dim


def _params(*sem):
    return pltpu.CompilerParams(dimension_semantics=sem, vmem_limit_bytes=VMEM_LIMIT)


def _silu(x):
    return x * jax.nn.sigmoid(x)


def _gelu_tanh(x):
    c = 0.7978845608028654
    return 0.5 * x * (1.0 + jnp.tanh(c * (x + 0.044715 * (x * x * x))))


def _mod_kernel(c_ref, w_ref, b_ref, o_ref):
    s = _silu(c_ref[...]).astype(BF16)
    o_ref[...] = jnp.dot(s, w_ref[...].astype(BF16), preferred_element_type=F32) + b_ref[...]


def _modulation(cond, w_mod, b_mod):
    rows, d = cond.shape
    n = w_mod.shape[1]
    tn = _tile(n, 512)
    return pl.pallas_call(
        _mod_kernel,
        grid=(n // tn,),
        in_specs=[pl.BlockSpec((rows, d), lambda j: (0, 0)),
                  pl.BlockSpec((d, tn), lambda j: (0, j)),
                  pl.BlockSpec((1, tn), lambda j: (0, j))],
        out_specs=pl.BlockSpec((rows, tn), lambda j: (0, j)),
        out_shape=jax.ShapeDtypeStruct((rows, n), F32),
        compiler_params=_params("parallel"),
        name="modulation",
    )(cond, w_mod, b_mod.reshape(1, n))


def _rms_mod(x, g, sc, sh):
    y = x * lax.rsqrt(jnp.mean(x * x, axis=-1, keepdims=True) + EPS)
    return (y * g) * (1.0 + sc) + sh


def _normmod_kernel(x_ref, g_ref, sc_ref, sh_ref, o_ref):
    o_ref[0] = _rms_mod(x_ref[0], g_ref[...], sc_ref[0], sh_ref[0]).astype(o_ref.dtype)


def _norm_modulate(x, g, sc, sh, row_of_batch):
    b, l, d = x.shape
    tl = _tile(l, 512, SUBLANES)
    return pl.pallas_call(
        _normmod_kernel,
        grid=(b, l // tl),
        in_specs=[pl.BlockSpec((1, tl, d), lambda i, j: (i, j, 0)),
                  pl.BlockSpec((1, d), lambda i, j: (0, 0)),
                  pl.BlockSpec((1, 1, d), lambda i, j: (row_of_batch(i), 0, 0)),
                  pl.BlockSpec((1, 1, d), lambda i, j: (row_of_batch(i), 0, 0))],
        out_specs=pl.BlockSpec((1, tl, d), lambda i, j: (i, j, 0)),
        out_shape=jax.ShapeDtypeStruct((b, l, d), BF16),
        compiler_params=_params("parallel", "parallel"),
        name="norm_modulate",
    )(x, g.reshape(1, d), sc, sh)


def _mm_kernel(a_ref, w_ref, o_ref):
    o_ref[...] = jnp.dot(a_ref[...], w_ref[...], preferred_element_type=F32).astype(o_ref.dtype)


def _matmul(a, w, out_dtype, tm_pref=1024, tn_pref=1024):
    m, k = a.shape
    n = w.shape[1]
    tm = _tile(m, tm_pref, SUBLANES)
    tn = _tile(n, tn_pref)
    return pl.pallas_call(
        _mm_kernel,
        grid=(m // tm, n // tn),
        in_specs=[pl.BlockSpec((tm, k), lambda i, j: (i, 0)),
                  pl.BlockSpec((k, tn), lambda i, j: (0, j))],
        out_specs=pl.BlockSpec((tm, tn), lambda i, j: (i, j)),
        out_shape=jax.ShapeDtypeStruct((m, n), out_dtype),
        compiler_params=_params("parallel", "parallel"),
        name="matmul",
    )(a, w)


def _conv_kernel(prev_ref, cur_ref, next_ref, w_ref, o_ref, *, tl, n_t, tc, k_tiles, conv_k):
    i = pl.program_id(1)
    j = pl.program_id(2)
    prev = jnp.where(i > 0, prev_ref[0], 0.0)
    nxt = jnp.where(i < n_t - 1, next_ref[0], 0.0)
    xx = jnp.concatenate([prev, cur_ref[0], nxt], axis=0)
    n = tl + 2 * SUBLANES
    acc = None
    for jj in range(conv_k):
        s = jj - conv_k // 2
        shifted = xx if s == 0 else pltpu.roll(xx, (-s) % n, 0)
        term = shifted[SUBLANES:SUBLANES + tl] * w_ref[jj:jj + 1, :]
        acc = term if acc is None else acc + term
    y = _silu(acc)
    is_k = j < k_tiles
    is_q = j >= 2 * k_tiles
    outs = []
    for hh in range(tc // HEAD):
        yh = y[:, hh * HEAD:(hh + 1) * HEAD]
        inv = lax.rsqrt(jnp.sum(yh * yh, axis=-1, keepdims=True) + EPS)
        scale = jnp.where(is_k, inv, jnp.where(is_q, inv * (HEAD ** -0.5), 1.0))
        outs.append(yh * scale)
    o_ref[0] = jnp.concatenate(outs, axis=-1)


def _short_conv(p, conv_w, width, n_sections):
    b, l, _ = p.shape
    conv_k = conv_w.shape[0]
    c = n_sections * width
    tc = _tile(width, 512)
    tl = _tile(l, 512, SUBLANES)
    n_t = l // tl
    sub = tl // SUBLANES
    kern = functools.partial(_conv_kernel, tl=tl, n_t=n_t, tc=tc, k_tiles=width // tc, conv_k=conv_k)
    return pl.pallas_call(
        kern,
        grid=(b, n_t, c // tc),
        in_specs=[
            pl.BlockSpec((1, SUBLANES, tc), lambda bi, i, j: (bi, jnp.maximum(i * sub - 1, 0), j)),
            pl.BlockSpec((1, tl, tc), lambda bi, i, j: (bi, i, j)),
            pl.BlockSpec((1, SUBLANES, tc), lambda bi, i, j: (bi, jnp.minimum((i + 1) * sub, l // SUBLANES - 1), j)),
            pl.BlockSpec((conv_k, tc), lambda bi, i, j: (0, j)),
        ],
        out_specs=pl.BlockSpec((1, tl, tc), lambda bi, i, j: (bi, i, j)),
        out_shape=jax.ShapeDtypeStruct((b, l, c), F32),
        compiler_params=_params("parallel", "parallel", "parallel"),
        name="short_conv",
    )(p, p, p, conv_w[:, :c])


def _decay_kernel(raw_ref, alog_ref, dtb_ref, cols_ref, dmat_ref, *, n_heads):
    h = n_heads
    c = DN_CHUNK
    raw = raw_ref[0]
    x = raw + dtb_ref[...]
    softplus = jnp.maximum(x, 0.0) + jnp.log1p(jnp.exp(-jnp.abs(x)))
    gs = -jnp.exp(alog_ref[...]) * softplus
    beta = jax.nn.sigmoid(raw)
    ii = lax.broadcasted_iota(I32, (c, c), 0)
    jj = lax.broadcasted_iota(I32, (c, c), 1)
    tri = [jj <= ii, jj >= ii]
    trif = [t.astype(F32) for t in tri]
    eye = (lax.broadcasted_iota(I32, (LANES, LANES), 0) == lax.broadcasted_iota(I32, (LANES, LANES), 1)).astype(F32)
    gs_t = lax.dot_general(eye, gs, NT_DIMS, precision=HIGHEST, preferred_element_type=F32)
    lane = lax.broadcasted_iota(I32, (c, LANES), 1)
    g_dir = [jnp.dot(trif[d], gs, precision=HIGHEST, preferred_element_type=F32) for d in range(2)]
    g_cum = jnp.where(lane < h, g_dir[0], g_dir[1])
    g_last = jnp.where(lane < h, g_dir[0][c - 1:c, :], g_dir[1][0:1, :])
    cols_ref[0] = jnp.concatenate([beta, jnp.exp(g_cum), jnp.exp(g_last - g_cum), jnp.exp(g_last)], axis=-1)
    for d in range(2):
        g_row = lax.dot_general(gs_t, trif[d], NT_DIMS, precision=HIGHEST, preferred_element_type=F32)
        mats = []
        for hh in range(h):
            col = d * h + hh
            diff = g_dir[d][:, col:col + 1] - g_row[col:col + 1, :]
            mats.append(jnp.where(tri[d], jnp.exp(jnp.where(tri[d], diff, 0.0)), 0.0))
        dmat_ref[0, d] = jnp.concatenate(mats, axis=-1)


def _decay_prep(raw, a_log, dt_bias, n_heads):
    b, l, _ = raw.shape
    h = n_heads
    pad = LANES - 2 * h
    alog = jnp.concatenate([a_log.reshape(1, 2 * h), jnp.zeros((1, pad), F32)], axis=-1)
    dtb = jnp.concatenate([dt_bias.reshape(1, 2 * h), jnp.zeros((1, pad), F32)], axis=-1)
    n = l // DN_CHUNK
    return pl.pallas_call(
        functools.partial(_decay_kernel, n_heads=h),
        grid=(b, n),
        in_specs=[pl.BlockSpec((1, DN_CHUNK, LANES), lambda i, j: (i, j, 0)),
                  pl.BlockSpec((1, LANES), lambda i, j: (0, 0)),
                  pl.BlockSpec((1, LANES), lambda i, j: (0, 0))],
        out_specs=[pl.BlockSpec((1, DN_CHUNK, 4 * LANES), lambda i, j: (i, j, 0)),
                   pl.BlockSpec((1, 2, DN_CHUNK, h * DN_CHUNK), lambda i, j: (i, 0, j, 0))],
        out_shape=[jax.ShapeDtypeStruct((b, l, 4 * LANES), F32),
                   jax.ShapeDtypeStruct((b, 2, l, h * DN_CHUNK), F32)],
        compiler_params=_params("parallel", "parallel"),
        name="decay_prep",
    )(raw, alog, dtb)


def _dot16(a, b, dims=None):
    a = a.astype(BF16)
    b = b.astype(BF16)
    if dims is None:
        return jnp.dot(a, b, preferred_element_type=F32)
    return lax.dot_general(a, b, dims, preferred_element_type=F32)


def _unit_triangular_inverses(mats, ii, jj):
    eye = (ii == jj).astype(F32)
    blk8 = (ii // 8) == (jj // 8)
    a0 = [jnp.where(blk8, a, 0.0) for a in mats]
    a2 = [_dot16(x, x) for x in a0]
    a4 = [_dot16(x, x) for x in a2]
    p = [eye - x for x in a0]
    p = [x + _dot16(x, y) for x, y in zip(p, a2)]
    p = [x + _dot16(x, y) for x, y in zip(p, a4)]
    for s in (8, 16, 32):
        sel = ((ii // (2 * s)) == (jj // (2 * s))) & ((ii // s) != (jj // s))
        t = [_dot16(jnp.where(sel, a, 0.0), x) for a, x in zip(mats, p)]
        p = [x - _dot16(x, y) for x, y in zip(p, t)]
    return p


def _delta_kernel(*refs, hb, n_heads, with_q, cps):
    if with_q:
        (kf_ref, vf_ref, qf_ref, kb_ref, vb_ref, qb_ref, df_ref, db_ref, cf_ref, cb_ref, s0_ref,
         of_ref, ob_ref, sfin_ref, s_ref) = refs
        q_refs = (qf_ref, qb_ref)
        o_refs = (of_ref, ob_ref)
    else:
        (kf_ref, vf_ref, kb_ref, vb_ref, df_ref, db_ref, cf_ref, cb_ref, s0_ref, sfin_ref, s_ref) = refs
    k_refs = (kf_ref, kb_ref)
    v_refs = (vf_ref, vb_ref)
    d_refs = (df_ref, db_ref)
    c_refs = (cf_ref, cb_ref)
    hg = pl.program_id(1)
    n = pl.program_id(2)
    c = DN_CHUNK

    @pl.when(n == 0)
    def _():
        s_ref[...] = s0_ref[0]

    ii = lax.broadcasted_iota(I32, (c, c), 0)
    jj = lax.broadcasted_iota(I32, (c, c), 1)
    lane = lax.broadcasted_iota(I32, (c, LANES), 1)

    def column(x, idx):
        return jnp.sum(jnp.where(lane == idx, x, 0.0), axis=-1, keepdims=True)

    chains = [(d, hh) for d in range(2) for hh in range(hb)]
    items = [(d, hh, ci) for ci in range(cps) for d, hh in chains]

    def rows_of(d, ci):
        r0 = (ci if d == 0 else cps - 1 - ci) * c
        return slice(r0, r0 + c)

    beta, eg, ekend, eglast, k, v, dm = [], [], [], [], [], [], []
    for d, hh, ci in items:
        head = hg * hb + hh
        cols = c_refs[d][0][rows_of(d, ci), :]
        beta.append(column(cols[:, 0:LANES], (2 + d) * n_heads + head))
        eg.append(column(cols[:, LANES:2 * LANES], d * n_heads + head))
        ekend.append(column(cols[:, 2 * LANES:3 * LANES], d * n_heads + head))
        eglast.append(column(cols[:, 3 * LANES:4 * LANES], d * n_heads + head)[0:1, :])
        k.append(k_refs[d][0][rows_of(d, ci), hh * HEAD:(hh + 1) * HEAD])
        v.append(v_refs[d][0][rows_of(d, ci), hh * HEAD:(hh + 1) * HEAD])
        dm.append(d_refs[d][0, 0][rows_of(d, ci), hh * c:(hh + 1) * c])
    strict = [(jj < ii) if d == 0 else (jj > ii) for d, _, _ in items]
    kb = [x * y for x, y in zip(k, beta)]
    kk = [_dot16(x, y, NT_DIMS) for x, y in zip(kb, k)]
    a = [jnp.where(m, x * y, 0.0) for m, x, y in zip(strict, kk, dm)]
    t = _unit_triangular_inverses(a, ii, jj)
    u = [_dot16(x, y * z) for x, y, z in zip(t, v, beta)]
    w = [_dot16(x, y * z) for x, y, z in zip(t, kb, eg)]
    ke = [x * y for x, y in zip(k, ekend)]
    if with_q:
        q = [q_refs[d][0][rows_of(d, ci), hh * HEAD:(hh + 1) * HEAD] for d, hh, ci in items]
        qk = [_dot16(x, y, NT_DIMS) * z for x, y, z in zip(q, k, dm)]
        qd = [x * y for x, y in zip(q, eg)]
        o = {}
    s = [s_ref[d, hh] for d, hh in chains]
    for ci in range(cps):
        idx = [ci * len(chains) + n_c for n_c in range(len(chains))]
        v_new = [u[i] - _dot16(w[i], x) for i, x in zip(idx, s)]
        if with_q:
            for i, x, vn, (d, hh) in zip(idx, s, v_new, chains):
                o[(d, hh, ci)] = _dot16(qd[i], x) + _dot16(qk[i], vn)
        upd = [_dot16(ke[i], vn, TN_DIMS) for i, vn in zip(idx, v_new)]
        s = [x * eglast[i] + z for i, x, z in zip(idx, s, upd)]
    for (d, hh), x in zip(chains, s):
        s_ref[d, hh] = x
    if with_q:
        for d in range(2):
            order = range(cps) if d == 0 else range(cps - 1, -1, -1)
            o_refs[d][0] = jnp.concatenate(
                [jnp.concatenate([o[(d, hh, ci)] for hh in range(hb)], axis=-1) for ci in order], axis=0)

    @pl.when(n == pl.num_programs(2) - 1)
    def _():
        sfin_ref[0] = s_ref[...]


def _delta_scan(kvq, cols, dmat, s0, n_heads, with_q):
    b, l, _ = kvq.shape
    h = n_heads
    hb = 4 if h % 4 == 0 else (2 if h % 2 == 0 else 1)
    n_chunks = l // DN_CHUNK
    cps = 4 if n_chunks % 4 == 0 else (2 if n_chunks % 2 == 0 else 1)
    c = cps * DN_CHUNK
    n = l // c
    wblk = h // hb

    def sec(section, rev):
        def imap(bi, g, j):
            return (bi, (n - 1 - j) if rev else j, section * wblk + g)
        return pl.BlockSpec((1, c, hb * HEAD), imap)

    def dspec(d):
        return pl.BlockSpec((1, 1, c, hb * DN_CHUNK), lambda bi, g, j: (bi, d, (n - 1 - j) if d else j, g))

    def cspec(d):
        return pl.BlockSpec((1, c, 4 * LANES), lambda bi, g, j: (bi, (n - 1 - j) if d else j, 0))

    state_spec = pl.BlockSpec((1, 2, hb, HEAD, HEAD), lambda bi, g, j: (bi, 0, g, 0, 0))
    n_sec = 3 if with_q else 2
    in_specs = [sec(s, False) for s in range(n_sec)] + [sec(s, True) for s in range(n_sec)]
    in_specs += [dspec(0), dspec(1), cspec(0), cspec(1), state_spec]
    args = [kvq] * (2 * n_sec) + [dmat, dmat, cols, cols, s0]
    out_specs = [state_spec]
    out_shape = [jax.ShapeDtypeStruct((b, 2, h, HEAD, HEAD), F32)]
    if with_q:
        ospec = [pl.BlockSpec((1, c, hb * HEAD), lambda bi, g, j: (bi, j, g)),
                 pl.BlockSpec((1, c, hb * HEAD), lambda bi, g, j: (bi, n - 1 - j, g))]
        out_specs = ospec + out_specs
        out_shape = [jax.ShapeDtypeStruct((b, l, h * HEAD), F32)] * 2 + out_shape
    res = pl.pallas_call(
        functools.partial(_delta_kernel, hb=hb, n_heads=h, with_q=with_q, cps=cps),
        grid=(b, wblk, n),
        in_specs=in_specs,
        out_specs=out_specs,
        out_shape=out_shape,
        scratch_shapes=[pltpu.VMEM((2, hb, HEAD, HEAD), F32)],
        compiler_params=_params("parallel", "parallel", "arbitrary"),
        name="delta_scan_q" if with_q else "delta_scan_state",
    )(*args)
    if with_q:
        return res[0], res[1], res[2]
    return None, None, res[0]


def _gated_norm_kernel(of_ref, ob_ref, z_ref, g_ref, o_ref):
    o = of_ref[0] + ob_ref[0]
    z = z_ref[0]
    outs = []
    for hh in range(o.shape[-1] // HEAD):
        oh = o[:, hh * HEAD:(hh + 1) * HEAD]
        y = oh * lax.rsqrt(jnp.mean(oh * oh, axis=-1, keepdims=True) + EPS) * g_ref[...]
        outs.append(y * _silu(z[:, hh * HEAD:(hh + 1) * HEAD]))
    o_ref[0] = jnp.concatenate(outs, axis=-1).astype(o_ref.dtype)


def _gated_norm(o_f, o_b, p, z_col, onorm_g):
    b, l, w = o_f.shape
    tl = _tile(l, 512, SUBLANES)
    tc = _tile(w, 512)
    zb = z_col // tc
    return pl.pallas_call(
        _gated_norm_kernel,
        grid=(b, l // tl, w // tc),
        in_specs=[pl.BlockSpec((1, tl, tc), lambda i, j, k: (i, j, k)),
                  pl.BlockSpec((1, tl, tc), lambda i, j, k: (i, j, k)),
                  pl.BlockSpec((1, tl, tc), lambda i, j, k: (i, j, zb + k)),
                  pl.BlockSpec((1, HEAD), lambda i, j, k: (0, 0))],
        out_specs=pl.BlockSpec((1, tl, tc), lambda i, j, k: (i, j, k)),
        out_shape=jax.ShapeDtypeStruct((b, l, w), BF16),
        compiler_params=_params("parallel", "parallel", "parallel"),
        name="gated_norm",
    )(o_f, o_b, p, onorm_g.reshape(1, HEAD))


def _gelu_ln_kernel(x_ref, g_ref, b_ref, o_ref):
    x = _gelu_tanh(x_ref[0])
    mu = jnp.mean(x, axis=-1, keepdims=True)
    xc = x - mu
    var = jnp.mean(xc * xc, axis=-1, keepdims=True)
    o_ref[0] = (xc * lax.rsqrt(var + EPS)) * g_ref[...] + b_ref[...]


def _gelu_layernorm(p, col, width, g, bias):
    b, l, _ = p.shape
    tl = _tile(l, 256, SUBLANES)
    cb = col // width
    return pl.pallas_call(
        _gelu_ln_kernel,
        grid=(b, l // tl),
        in_specs=[pl.BlockSpec((1, tl, width), lambda i, j: (i, j, cb)),
                  pl.BlockSpec((1, width), lambda i, j: (0, 0)),
                  pl.BlockSpec((1, width), lambda i, j: (0, 0))],
        out_specs=pl.BlockSpec((1, tl, width), lambda i, j: (i, j, 0)),
        out_shape=jax.ShapeDtypeStruct((b, l, width), F32),
        compiler_params=_params("parallel", "parallel"),
        name="gelu_layernorm",
    )(p, g.reshape(1, width), bias.reshape(1, width))


def _spatial_kernel(gv_ref, gu_ref, ws_ref, bs_ref, o_ref, *, seq, row_groups, grid_w):
    g = pl.program_id(1)
    w = ws_ref[0].astype(BF16)
    bias = bs_ref[0]
    n_chunks = seq // GM_CHUNK
    rows = seq // grid_w
    cols_per_chunk = GM_CHUNK // rows

    @pl.when(g < row_groups)
    def _():
        for n in range(n_chunks):
            sl = pl.ds(n * GM_CHUNK, GM_CHUNK)
            s = jnp.dot(w, gv_ref[0, sl, :].astype(BF16), preferred_element_type=F32) + bias
            o_ref[0, sl, :] = (_gelu_tanh(gu_ref[0, sl, :]) * s).astype(o_ref.dtype)

    @pl.when(g >= row_groups)
    def _():
        for n in range(n_chunks):
            sls = [pl.ds(n * cols_per_chunk + cc, rows, stride=grid_w) for cc in range(cols_per_chunk)]
            v = jnp.concatenate([gv_ref[0, sl, :] for sl in sls], axis=0)
            s = jnp.dot(w, v.astype(BF16), preferred_element_type=F32) + bias
            for cc, sl in enumerate(sls):
                o_ref[0, sl, :] = (_gelu_tanh(gu_ref[0, sl, :]) * s[cc * rows:(cc + 1) * rows]).astype(o_ref.dtype)


def _spatial_gate(gvn, p, gu_col, gm_ws, gm_bs):
    b, l, width = gvn.shape
    groups = gm_ws.shape[0]
    gub = gu_col // HEAD
    kern = functools.partial(_spatial_kernel, seq=l, row_groups=groups // 2, grid_w=GRID_W)
    return pl.pallas_call(
        kern,
        grid=(b, groups),
        in_specs=[pl.BlockSpec((1, l, HEAD), lambda i, g: (i, 0, g)),
                  pl.BlockSpec((1, l, HEAD), lambda i, g: (i, 0, gub + g)),
                  pl.BlockSpec((1, GM_CHUNK, GM_CHUNK), lambda i, g: (g, 0, 0)),
                  pl.BlockSpec((1, GM_CHUNK, 1), lambda i, g: (g, 0, 0))],
        out_specs=pl.BlockSpec((1, l, HEAD), lambda i, g: (i, 0, g)),
        out_shape=jax.ShapeDtypeStruct((b, l, width), F32),
        compiler_params=_params("parallel", "parallel"),
        name="spatial_gate",
    )(gvn, p, gm_ws, gm_bs.reshape(groups, GM_CHUNK, 1))


def _merge_kernel(ya_ref, yb_ref, wa_ref, wb_ref, ga_ref, gb_ref, o_ref):
    pa = jnp.dot(ya_ref[...], wa_ref[...], preferred_element_type=F32)
    pb = jnp.dot(yb_ref[...].astype(BF16), wb_ref[...], preferred_element_type=F32)
    o_ref[...] = (jax.nn.sigmoid(ga_ref[...]) * pa + jax.nn.sigmoid(gb_ref[...]) * pb).astype(o_ref.dtype)


def _merge(ya, yb, wa, wb, p2d, merge_col):
    m, ka = ya.shape
    kb = yb.shape[1]
    d = wa.shape[1]
    tm = _tile(m, 512, SUBLANES)
    tn = _tile(d, 1024)
    ca = merge_col // tn
    cb = (merge_col + d) // tn
    return pl.pallas_call(
        _merge_kernel,
        grid=(m // tm, d // tn),
        in_specs=[pl.BlockSpec((tm, ka), lambda i, j: (i, 0)),
                  pl.BlockSpec((tm, kb), lambda i, j: (i, 0)),
                  pl.BlockSpec((ka, tn), lambda i, j: (0, j)),
                  pl.BlockSpec((kb, tn), lambda i, j: (0, j)),
                  pl.BlockSpec((tm, tn), lambda i, j: (i, ca + j)),
                  pl.BlockSpec((tm, tn), lambda i, j: (i, cb + j))],
        out_specs=pl.BlockSpec((tm, tn), lambda i, j: (i, j)),
        out_shape=jax.ShapeDtypeStruct((m, d), BF16),
        compiler_params=_params("parallel", "parallel"),
        name="merge",
    )(ya, yb, wa, wb, p2d, p2d)


def _oproj_kernel(a_ref, w_ref, x_ref, gt_ref, o_ref):
    y = jnp.dot(a_ref[...], w_ref[...], preferred_element_type=F32)
    o_ref[...] = x_ref[...] + gt_ref[0] * y


def _out_proj_residual(a, w, x2d, gate, seq):
    m, k = a.shape
    d = w.shape[1]
    tm = _tile(seq, 1024, SUBLANES)
    tn = _tile(d, 1024)
    per_batch = seq // tm
    return pl.pallas_call(
        _oproj_kernel,
        grid=(m // tm, d // tn),
        in_specs=[pl.BlockSpec((tm, k), lambda i, j: (i, 0)),
                  pl.BlockSpec((k, tn), lambda i, j: (0, j)),
                  pl.BlockSpec((tm, tn), lambda i, j: (i, j)),
                  pl.BlockSpec((1, 1, tn), lambda i, j: (i // per_batch, 0, j))],
        out_specs=pl.BlockSpec((tm, tn), lambda i, j: (i, j)),
        out_shape=jax.ShapeDtypeStruct((m, d), F32),
        compiler_params=_params("parallel", "parallel"),
        name="out_proj",
    )(a, w, x2d, gate)


def _router_kernel(x_ref, g_ref, sc_ref, sh_ref, wr_ref, br_ref, h_ref, ti_ref, tw_ref, rk_ref, cnt_ref, carry,
                   *, n_exp, tm):
    i = pl.program_id(0)

    @pl.when(i == 0)
    def _():
        carry[...] = jnp.zeros_like(carry)

    h2 = _rms_mod(x_ref[...], g_ref[...], sc_ref[0], sh_ref[0])
    h_hi = h2.astype(BF16)
    h_lo = (h2 - h_hi.astype(F32)).astype(BF16)
    half = h2.shape[1] // 2
    bits = lax.bitcast_convert_type(h_hi.astype(F32), U32)
    h_ref[...] = bits[:, :half] | (bits[:, half:] >> 16)
    wr = wr_ref[...]
    w_hi = wr.astype(BF16)
    w_lo = (wr - w_hi.astype(F32)).astype(BF16)
    logits = (jnp.dot(h_hi, w_hi, preferred_element_type=F32) + jnp.dot(h_hi, w_lo, preferred_element_type=F32)
              + jnp.dot(h_lo, w_hi, preferred_element_type=F32)) + br_ref[...]
    lane = lax.broadcasted_iota(I32, (tm, n_exp), 1)
    work = logits
    member = jnp.zeros((tm, n_exp), F32)
    vals, ids = [], []
    for _ in range(TOP_K):
        mx = jnp.max(work, axis=-1, keepdims=True)
        idx = jnp.min(jnp.where(work == mx, lane, n_exp), axis=-1, keepdims=True)
        hit = lane == idx
        vals.append(mx)
        ids.append(idx)
        work = jnp.where(hit, -jnp.inf, work)
        member = member + hit.astype(F32)
    exps = [jnp.exp(v - vals[0]) for v in vals]
    den = exps[0]
    for e in exps[1:]:
        den = den + e
    rr = lax.broadcasted_iota(I32, (tm, tm), 0)
    cc = lax.broadcasted_iota(I32, (tm, tm), 1)
    below = (cc < rr).astype(BF16)
    rank_all = jnp.dot(below, member.astype(BF16), preferred_element_type=F32) + carry[...]
    out_lane = lax.broadcasted_iota(I32, (tm, LANES), 1)
    ti = jnp.zeros((tm, LANES), I32)
    tw = jnp.zeros((tm, LANES), F32)
    rk = jnp.zeros((tm, LANES), F32)
    for k in range(TOP_K):
        rank_k = jnp.sum(jnp.where(lane == ids[k], rank_all, 0.0), axis=-1, keepdims=True)
        ti = jnp.where(out_lane == k, ids[k], ti)
        tw = jnp.where(out_lane == k, exps[k] / den, tw)
        rk = jnp.where(out_lane == k, rank_k, rk)
    ti_ref[...] = ti
    tw_ref[...] = tw
    rk_ref[...] = rk.astype(I32)
    carry[...] = carry[...] + jnp.sum(member, axis=0, keepdims=True)
    cnt_ref[...] = carry[...]


def _router(x2d, g, sc, sh, w_router, b_router, seq):
    m, d = x2d.shape
    n_exp = w_router.shape[1]
    tm = _tile(seq, 256, SUBLANES)
    per_batch = seq // tm
    kern = functools.partial(_router_kernel, n_exp=n_exp, tm=tm)
    row = lambda i: (i, 0)
    return pl.pallas_call(
        kern,
        grid=(m // tm,),
        in_specs=[pl.BlockSpec((tm, d), row),
                  pl.BlockSpec((1, d), lambda i: (0, 0)),
                  pl.BlockSpec((1, 1, d), lambda i: (i // per_batch, 0, 0)),
                  pl.BlockSpec((1, 1, d), lambda i: (i // per_batch, 0, 0)),
                  pl.BlockSpec((d, n_exp), lambda i: (0, 0)),
                  pl.BlockSpec((1, n_exp), lambda i: (0, 0))],
        out_specs=[pl.BlockSpec((tm, d // 2), row),
                   pl.BlockSpec((tm, LANES), row),
                   pl.BlockSpec((tm, LANES), row),
                   pl.BlockSpec((tm, LANES), row),
                   pl.BlockSpec((1, n_exp), lambda i: (0, 0))],
        out_shape=[jax.ShapeDtypeStruct((m, d // 2), U32),
                   jax.ShapeDtypeStruct((m, LANES), I32),
                   jax.ShapeDtypeStruct((m, LANES), F32),
                   jax.ShapeDtypeStruct((m, LANES), I32),
                   jax.ShapeDtypeStruct((1, n_exp), F32)],
        scratch_shapes=[pltpu.VMEM((1, n_exp), F32)],
        compiler_params=_params("arbitrary"),
        name="router",
    )(x2d, g.reshape(1, d), sc, sh, w_router, b_router.reshape(1, n_exp))


def _dispatch_kernel(state_ref, tok_ref, h_hbm, xs_ref, buf, sem, *, rows):
    live = state_ref[pl.program_id(0)] != BLK_UNUSED

    @pl.when(live)
    def _():
        group = 8

        def start(g, carry):
            for u in range(group):
                i = g * group + u
                pltpu.make_async_copy(h_hbm.at[pl.ds(tok_ref[i], 1)], buf.at[pl.ds(i, 1)], sem).start(priority=u % 2)
            return carry

        lax.fori_loop(0, rows // group, start, 0)
        pltpu.make_async_copy(h_hbm.at[pl.ds(0, rows)], buf, sem).wait()
        bits = buf[...]
        half = bits.shape[1]
        hi = lax.bitcast_convert_type(bits & jnp.uint32(0xFFFF0000), F32)
        lo = lax.bitcast_convert_type(bits << 16, F32)
        xs_ref[:, :half] = hi.astype(xs_ref.dtype)
        xs_ref[:, half:] = lo.astype(xs_ref.dtype)

    @pl.when(jnp.logical_not(live))
    def _():
        xs_ref[...] = jnp.zeros_like(xs_ref)


def _dispatch(h_packed, slot_tok, blk_state):
    n_slots = slot_tok.shape[0]
    half = h_packed.shape[1]
    rows = MOE_ROWS
    grid_spec = pltpu.PrefetchScalarGridSpec(
        num_scalar_prefetch=1,
        grid=(n_slots // rows,),
        in_specs=[pl.BlockSpec((rows,), lambda i, st: (i,), memory_space=pltpu.SMEM),
                  pl.BlockSpec(memory_space=pl.ANY)],
        out_specs=pl.BlockSpec((rows, 2 * half), lambda i, st: (i, 0)),
        scratch_shapes=[pltpu.VMEM((rows, half), U32), pltpu.SemaphoreType.DMA(())],
    )
    return pl.pallas_call(
        functools.partial(_dispatch_kernel, rows=rows),
        grid_spec=grid_spec,
        out_shape=jax.ShapeDtypeStruct((n_slots, 2 * half), BF16),
        compiler_params=_params("arbitrary"),
        name="dispatch",
    )(blk_state, slot_tok, h_packed)


def _ffn1_kernel(be_ref, state_ref, xs_ref, wg_ref, wl_ref, bg_ref, bl_ref, o_ref, wg16, wl16):
    del be_ref
    state = state_ref[pl.program_id(1)]

    @pl.when(state == BLK_NEW_EXPERT)
    def _():
        wg16[...] = wg_ref[0].astype(BF16)
        wl16[...] = wl_ref[0].astype(BF16)

    @pl.when(state != BLK_UNUSED)
    def _():
        x = xs_ref[...]
        glu = jnp.dot(x, wg16[...], preferred_element_type=F32) + bg_ref[0]
        lin = jnp.dot(x, wl16[...], preferred_element_type=F32) + bl_ref[0]
        glu = jnp.minimum(glu, SWIGLU_LIMIT)
        lin = jnp.clip(lin, -SWIGLU_LIMIT, SWIGLU_LIMIT)
        o_ref[...] = (glu * jax.nn.sigmoid(SWIGLU_ALPHA * glu) * (lin + 1.0)).astype(o_ref.dtype)

    @pl.when(state == BLK_UNUSED)
    def _():
        o_ref[...] = jnp.zeros_like(o_ref)


def _ffn1(xs, w1, b1, blk_expert, blk_new):
    n_slots, d = xs.shape
    n_exp, _, two_f = w1.shape
    f = two_f // 2
    rows = MOE_ROWS
    tn = _tile(f, 512)
    nj = f // tn
    grid_spec = pltpu.PrefetchScalarGridSpec(
        num_scalar_prefetch=2,
        grid=(nj, n_slots // rows),
        in_specs=[pl.BlockSpec((rows, d), lambda j, i, be, nw: (i, 0)),
                  pl.BlockSpec((1, d, tn), lambda j, i, be, nw: (be[i], 0, j)),
                  pl.BlockSpec((1, d, tn), lambda j, i, be, nw: (be[i], 0, nj + j)),
                  pl.BlockSpec((1, 1, tn), lambda j, i, be, nw: (be[i], 0, j)),
                  pl.BlockSpec((1, 1, tn), lambda j, i, be, nw: (be[i], 0, nj + j))],
        out_specs=pl.BlockSpec((rows, tn), lambda j, i, be, nw: (i, j)),
        scratch_shapes=[pltpu.VMEM((d, tn), BF16), pltpu.VMEM((d, tn), BF16)],
    )
    return pl.pallas_call(
        _ffn1_kernel,
        grid_spec=grid_spec,
        out_shape=jax.ShapeDtypeStruct((n_slots, f), BF16),
        compiler_params=_params("arbitrary", "arbitrary"),
        name="expert_ffn1",
    )(blk_expert, blk_new, xs, w1, w1, b1.reshape(n_exp, 1, two_f), b1.reshape(n_exp, 1, two_f))


def _ffn2_kernel(be_ref, state_ref, a_ref, w_ref, b_ref, o_ref, w16):
    del be_ref
    state = state_ref[pl.program_id(1)]

    @pl.when(state == BLK_NEW_EXPERT)
    def _():
        w16[...] = w_ref[0].astype(BF16)

    @pl.when(state != BLK_UNUSED)
    def _():
        o_ref[...] = jnp.dot(a_ref[...], w16[...], preferred_element_type=F32) + b_ref[0]

    @pl.when(state == BLK_UNUSED)
    def _():
        o_ref[...] = jnp.zeros_like(o_ref)


def _ffn2(act, w2, b2, blk_expert, blk_state):
    n_slots, f = act.shape
    n_exp, _, d = w2.shape
    rows = MOE_ROWS
    tn = _tile(d, 2048)
    grid_spec = pltpu.PrefetchScalarGridSpec(
        num_scalar_prefetch=2,
        grid=(d // tn, n_slots // rows),
        in_specs=[pl.BlockSpec((rows, f), lambda j, i, be, st: (i, 0)),
                  pl.BlockSpec((1, f, tn), lambda j, i, be, st: (be[i], 0, j)),
                  pl.BlockSpec((1, 1, tn), lambda j, i, be, st: (be[i], 0, j))],
        out_specs=pl.BlockSpec((rows, tn), lambda j, i, be, st: (i, j)),
        scratch_shapes=[pltpu.VMEM((f, tn), BF16)],
    )
    return pl.pallas_call(
        _ffn2_kernel,
        grid_spec=grid_spec,
        out_shape=jax.ShapeDtypeStruct((n_slots, d), F32),
        compiler_params=_params("arbitrary", "arbitrary"),
        name="expert_ffn2",
    )(blk_expert, blk_state, act, w2, b2.reshape(n_exp, 1, d))


def _combine_kernel(dest_ref, x_ref, tw_ref, gt_ref, gf_ref, ys_hbm, o_ref, buf, sem, *, rows):
    def start(t, carry):
        for k in range(TOP_K):
            src = ys_hbm.at[pl.ds(dest_ref[t * TOP_K + k], 1)]
            pltpu.make_async_copy(src, buf.at[k, pl.ds(t, 1)], sem).start(priority=k % 2)
        return carry

    lax.fori_loop(0, rows, start, 0, unroll=2)
    for k in range(TOP_K):
        pltpu.make_async_copy(ys_hbm.at[pl.ds(0, rows)], buf.at[k], sem).wait()
    tw = tw_ref[...]
    y = buf[0] * tw[:, 0:1]
    for k in range(1, TOP_K):
        y = y + buf[k] * tw[:, k:k + 1]
    x = x_ref[...] + gt_ref[0] * y
    o_ref[...] = x * lax.rsqrt(jnp.mean(x * x, axis=-1, keepdims=True) + EPS) * gf_ref[...]


def _combine(x1, tw, gate, normf_g, ys, dest_flat, seq):
    m, d = x1.shape
    rows = _tile(seq, 128, SUBLANES)
    per_batch = seq // rows
    return pl.pallas_call(
        functools.partial(_combine_kernel, rows=rows),
        grid=(m // rows,),
        in_specs=[pl.BlockSpec((rows * TOP_K,), lambda i: (i,), memory_space=pltpu.SMEM),
                  pl.BlockSpec((rows, d), lambda i: (i, 0)),
                  pl.BlockSpec((rows, LANES), lambda i: (i, 0)),
                  pl.BlockSpec((1, 1, d), lambda i: (i // per_batch, 0, 0)),
                  pl.BlockSpec((1, d), lambda i: (0, 0)),
                  pl.BlockSpec(memory_space=pl.ANY)],
        out_specs=pl.BlockSpec((rows, d), lambda i: (i, 0)),
        out_shape=jax.ShapeDtypeStruct((m, d), F32),
        scratch_shapes=[pltpu.VMEM((TOP_K, rows, d), F32), pltpu.SemaphoreType.DMA(())],
        compiler_params=_params("arbitrary"),
        name="combine",
    )(dest_flat, x1, tw, gate, normf_g.reshape(1, d), ys)


def kernel(x, c, ctx, c_ctx, w_mod, b_mod, norm1_g, w_in, conv_w, a_log, dt_bias, onorm_g, gm_ln_g, gm_ln_b,
           gm_ws, gm_bs, w_up_a, w_up_b, w_o, norm2_g, w_router, b_router, w1, b1, w2, b2, normf_g):
    depth = w_mod.shape[0]
    assert depth == 1, "single-layer block"
    bsz, seq, d = x.shape
    ctx_len = ctx.shape[1]
    n_heads = a_log.shape[-1]
    dn_w = n_heads * HEAD
    gm_w = gm_ln_g.shape[-1]
    n_exp = w_router.shape[-1]
    assert seq % GRID_W == 0 and seq % GM_CHUNK == 0 and seq % DN_CHUNK == 0 and ctx_len % DN_CHUNK == 0
    assert 4 * n_heads <= LANES and GM_CHUNK % (seq // GRID_W) == 0
    (w_mod, b_mod, norm1_g, w_in, conv_w, a_log, dt_bias, onorm_g, gm_ln_g, gm_ln_b, gm_ws, gm_bs, w_up_a, w_up_b,
     w_o, norm2_g, w_router, b_router, w1, b1, w2, b2) = (
        t[0] for t in (w_mod, b_mod, norm1_g, w_in, conv_w, a_log, dt_bias, onorm_g, gm_ln_g, gm_ln_b, gm_ws, gm_bs,
                       w_up_a, w_up_b, w_o, norm2_g, w_router, b_router, w1, b1, w2, b2))

    n_rows = -(-(bsz + 1) // SUBLANES) * SUBLANES
    cond = jnp.concatenate([c, c_ctx[None], jnp.zeros((n_rows - bsz - 1, d), F32)], axis=0)
    mod = _modulation(cond, w_mod, b_mod)
    sh1, sc1, gt1, sh2, sc2, gt2 = (mod[:, i * d:(i + 1) * d].reshape(n_rows, 1, d) for i in range(N_MOD))

    col_decay = 2 * dn_w
    col_q = col_decay + 4 * n_heads
    w_main = jnp.concatenate([w_in[:, :col_decay], w_in[:, col_q:]], axis=1).astype(BF16)
    w_dec = jnp.concatenate([w_in[:, col_decay:col_q], jnp.zeros((d, LANES - 4 * n_heads), F32)], axis=1).astype(BF16)
    col_z = 3 * dn_w
    col_gu = col_z + dn_w
    col_gv = col_gu + gm_w
    col_merge = col_gv + gm_w

    hc = _norm_modulate(ctx, norm1_g, sc1, sh1, lambda i: bsz).reshape(bsz * ctx_len, d)
    pc = _matmul(hc, w_main[:, :2 * dn_w], F32).reshape(bsz, ctx_len, 2 * dn_w)
    rawc = _matmul(hc, w_dec, F32).reshape(bsz, ctx_len, LANES)
    kvc = _short_conv(pc, conv_w, dn_w, 2)
    colsc, dmatc = _decay_prep(rawc, a_log, dt_bias, n_heads)
    zero_state = jnp.zeros((bsz, 2, n_heads, HEAD, HEAD), F32)
    _, _, ctx_state = _delta_scan(kvc, colsc, dmatc, zero_state, n_heads, False)

    h = _norm_modulate(x, norm1_g, sc1, sh1, lambda i: i).reshape(bsz * seq, d)
    p2d = _matmul(h, w_main, F32)
    p = p2d.reshape(bsz, seq, -1)
    raw = _matmul(h, w_dec, F32).reshape(bsz, seq, LANES)
    kvq = _short_conv(p, conv_w, dn_w, 3)
    cols, dmat = _decay_prep(raw, a_log, dt_bias, n_heads)
    o_f, o_b, _ = _delta_scan(kvq, cols, dmat, ctx_state, n_heads, True)
    y_a = _gated_norm(o_f, o_b, p, col_z, onorm_g).reshape(bsz * seq, dn_w)
    gvn = _gelu_layernorm(p, col_gv, gm_w, gm_ln_g, gm_ln_b)
    y_b = _spatial_gate(gvn, p, col_gu, gm_ws, gm_bs).reshape(bsz * seq, gm_w)
    merged = _merge(y_a, y_b, w_up_a.astype(BF16), w_up_b.astype(BF16), p2d, col_merge)
    x2d = x.reshape(bsz * seq, d)
    x1 = _out_proj_residual(merged, w_o.astype(BF16), x2d, gt1, seq)

    h2, ti, tw, rk, cnt = _router(x1, norm2_g, sc2, sh2, w_router, b_router, seq)
    n_tok = bsz * seq
    top_i = ti[:, :TOP_K]
    counts = cnt[0].astype(I32)
    padded = (counts + MOE_ROWS - 1) // MOE_ROWS * MOE_ROWS
    pad_end = jnp.cumsum(padded)
    pad_start = pad_end - padded
    dest = (pad_start[top_i] + rk[:, :TOP_K]).reshape(-1)
    n_slots = n_tok * TOP_K + n_exp * MOE_ROWS
    tok_flat = jnp.repeat(jnp.arange(n_tok, dtype=I32), TOP_K)
    slot_tok = jnp.zeros((n_slots,), I32).at[dest].set(tok_flat)
    blk_start = jnp.arange(n_slots // MOE_ROWS, dtype=I32) * MOE_ROWS
    blk_expert = jnp.minimum(jnp.sum((pad_end[None, :] <= blk_start[:, None]).astype(I32), axis=1), n_exp - 1)
    changed = jnp.concatenate([jnp.ones((1,), jnp.bool_), blk_expert[1:] != blk_expert[:-1]])
    blk_state = jnp.where(blk_start >= pad_end[-1], BLK_UNUSED,
                          jnp.where(changed, BLK_NEW_EXPERT, BLK_SAME_EXPERT)).astype(I32)
    xs = _dispatch(h2, slot_tok, blk_state)
    act = _ffn1(xs, w1, b1, blk_expert, blk_state)
    ys = _ffn2(act, w2, b2, blk_expert, blk_state)
    out = _combine(x1, tw, gt2, normf_g, ys, dest, seq)
    return out.reshape(bsz, seq, d)
```

```python
import functools

import jax
import jax.numpy as jnp
from jax import lax
from jax.experimental import pallas as pl
from jax.experimental.pallas import tpu as pltpu

F32 = jnp.float32
BF16 = jnp.bfloat16
I32 = jnp.int32
U32 = jnp.uint32
HIGHEST = lax.Precision.HIGHEST

EPS = 1e-6
N_MOD = 6
GRID_W = 64
HEAD = 128
DN_CHUNK = 64
GM_CHUNK = 128
TOP_K = 4
SWIGLU_LIMIT = 7.0
SWIGLU_ALPHA = 1.702
MOE_ROWS = 256
BLK_SAME_EXPERT, BLK_NEW_EXPERT, BLK_UNUSED = 0, 1, 2
LANES = 128
SUBLANES = 8
VMEM_LIMIT = 56 * 1024 * 1024

NT_DIMS = (((1,), (1,)), ((), ()))
TN_DIMS = (((0,), (0,)), ((), ()))


def _tile(dim, pref, align=LANES):
    t = min(pref, dim) // align * align
    while t >= align:
        if dim % t == 0:
            return t
        t -= align
    return dim


def _params(*sem):
    return pltpu.CompilerParams(dimension_semantics=sem, vmem_limit_bytes=VMEM_LIMIT)


def _silu(x):
    return x * jax.nn.sigmoid(x)


def _gelu_tanh(x):
    c = 0.7978845608028654
    return 0.5 * x * (1.0 + jnp.tanh(c * (x + 0.044715 * (x * x * x))))


def _mod_kernel(c_ref, w_ref, b_ref, o_ref):
    s = _silu(c_ref[...]).astype(BF16)
    o_ref[...] = jnp.dot(s, w_ref[...].astype(BF16), preferred_element_type=F32) + b_ref[...]


def _modulation(cond, w_mod, b_mod):
    rows, d = cond.shape
    n = w_mod.shape[1]
    tn = _tile(n, 512)
    return pl.pallas_call(
        _mod_kernel,
        grid=(n // tn,),
        in_specs=[pl.BlockSpec((rows, d), lambda j: (0, 0)),
                  pl.BlockSpec((d, tn), lambda j: (0, j)),
                  pl.BlockSpec((1, tn), lambda j: (0, j))],
        out_specs=pl.BlockSpec((rows, tn), lambda j: (0, j)),
        out_shape=jax.ShapeDtypeStruct((rows, n), F32),
        compiler_params=_params("parallel"),
        name="modulation",
    )(cond, w_mod, b_mod.reshape(1, n))


def _rms_mod(x, g, sc, sh):
    y = x * lax.rsqrt(jnp.mean(x * x, axis=-1, keepdims=True) + EPS)
    return (y * g) * (1.0 + sc) + sh


def _normmod_kernel(x_ref, g_ref, sc_ref, sh_ref, o_ref):
    o_ref[0] = _rms_mod(x_ref[0], g_ref[...], sc_ref[0], sh_ref[0]).astype(o_ref.dtype)


def _norm_modulate(x, g, sc, sh, row_of_batch):
    b, l, d = x.shape
    tl = _tile(l, 512, SUBLANES)
    return pl.pallas_call(
        _normmod_kernel,
        grid=(b, l // tl),
        in_specs=[pl.BlockSpec((1, tl, d), lambda i, j: (i, j, 0)),
                  pl.BlockSpec((1, d), lambda i, j: (0, 0)),
                  pl.BlockSpec((1, 1, d), lambda i, j: (row_of_batch(i), 0, 0)),
                  pl.BlockSpec((1, 1, d), lambda i, j: (row_of_batch(i), 0, 0))],
        out_specs=pl.BlockSpec((1, tl, d), lambda i, j: (i, j, 0)),
        out_shape=jax.ShapeDtypeStruct((b, l, d), BF16),
        compiler_params=_params("parallel", "parallel"),
        name="norm_modulate",
    )(x, g.reshape(1, d), sc, sh)


def _mm_kernel(a_ref, w_ref, o_ref):
    o_ref[...] = jnp.dot(a_ref[...], w_ref[...], preferred_element_type=F32).astype(o_ref.dtype)


def _matmul(a, w, out_dtype, tm_pref=1024, tn_pref=1024):
    m, k = a.shape
    n = w.shape[1]
    tm = _tile(m, tm_pref, SUBLANES)
    tn = _tile(n, tn_pref)
    return pl.pallas_call(
        _mm_kernel,
        grid=(m // tm, n // tn),
        in_specs=[pl.BlockSpec((tm, k), lambda i, j: (i, 0)),
                  pl.BlockSpec((k, tn), lambda i, j: (0, j))],
        out_specs=pl.BlockSpec((tm, tn), lambda i, j: (i, j)),
        out_shape=jax.ShapeDtypeStruct((m, n), out_dtype),
        compiler_params=_params("parallel", "parallel"),
        name="matmul",
    )(a, w)


def _conv_kernel(prev_ref, cur_ref, next_ref, w_ref, o_ref, *, tl, n_t, tc, k_tiles, conv_k):
    i = pl.program_id(1)
    j = pl.program_id(2)
    prev = jnp.where(i > 0, prev_ref[0], 0.0)
    nxt = jnp.where(i < n_t - 1, next_ref[0], 0.0)
    xx = jnp.concatenate([prev, cur_ref[0], nxt], axis=0)
    n = tl + 2 * SUBLANES
    acc = None
    for jj in range(conv_k):
        s = jj - conv_k // 2
        shifted = xx if s == 0 else pltpu.roll(xx, (-s) % n, 0)
        term = shifted[SUBLANES:SUBLANES + tl] * w_ref[jj:jj + 1, :]
        acc = term if acc is None else acc + term
    y = _silu(acc)
    is_k = j < k_tiles
    is_q = j >= 2 * k_tiles
    outs = []
    for hh in range(tc // HEAD):
        yh = y[:, hh * HEAD:(hh + 1) * HEAD]
        inv = lax.rsqrt(jnp.sum(yh * yh, axis=-1, keepdims=True) + EPS)
        scale = jnp.where(is_k, inv, jnp.where(is_q, inv * (HEAD ** -0.5), 1.0))
        outs.append(yh * scale)
    o_ref[0] = jnp.concatenate(outs, axis=-1)


def _short_conv(p, conv_w, width, n_sections):
    b, l, _ = p.shape
    conv_k = conv_w.shape[0]
    c = n_sections * width
    tc = _tile(width, 512)
    tl = _tile(l, 512, SUBLANES)
    n_t = l // tl
    sub = tl // SUBLANES
    kern = functools.partial(_conv_kernel, tl=tl, n_t=n_t, tc=tc, k_tiles=width // tc, conv_k=conv_k)
    return pl.pallas_call(
        kern,
        grid=(b, n_t, c // tc),
        in_specs=[
            pl.BlockSpec((1, SUBLANES, tc), lambda bi, i, j: (bi, jnp.maximum(i * sub - 1, 0), j)),
            pl.BlockSpec((1, tl, tc), lambda bi, i, j: (bi, i, j)),
            pl.BlockSpec((1, SUBLANES, tc), lambda bi, i, j: (bi, jnp.minimum((i + 1) * sub, l // SUBLANES - 1), j)),
            pl.BlockSpec((conv_k, tc), lambda bi, i, j: (0, j)),
        ],
        out_specs=pl.BlockSpec((1, tl, tc), lambda bi, i, j: (bi, i, j)),
        out_shape=jax.ShapeDtypeStruct((b, l, c), F32),
        compiler_params=_params("parallel", "parallel", "parallel"),
        name="short_conv",
    )(p, p, p, conv_w[:, :c])


def _decay_kernel(raw_ref, alog_ref, dtb_ref, cols_ref, dmat_ref, *, n_heads):
    h = n_heads
    c = DN_CHUNK
    raw = raw_ref[0]
    x = raw + dtb_ref[...]
    softplus = jnp.maximum(x, 0.0) + jnp.log1p(jnp.exp(-jnp.abs(x)))
    gs = -jnp.exp(alog_ref[...]) * softplus
    beta = jax.nn.sigmoid(raw)
    ii = lax.broadcasted_iota(I32, (c, c), 0)
    jj = lax.broadcasted_iota(I32, (c, c), 1)
    tri = [jj <= ii, jj >= ii]
    trif = [t.astype(F32) for t in tri]
    eye = (lax.broadcasted_iota(I32, (LANES, LANES), 0) == lax.broadcasted_iota(I32, (LANES, LANES), 1)).astype(F32)
    gs_t = lax.dot_general(eye, gs, NT_DIMS, precision=HIGHEST, preferred_element_type=F32)
    lane = lax.broadcasted_iota(I32, (c, LANES), 1)
    g_dir = [jnp.dot(trif[d], gs, precision=HIGHEST, preferred_element_type=F32) for d in range(2)]
    g_cum = jnp.where(lane < h, g_dir[0], g_dir[1])
    g_last = jnp.where(lane < h, g_dir[0][c - 1:c, :], g_dir[1][0:1, :])
    cols_ref[0] = jnp.concatenate([beta, jnp.exp(g_cum), jnp.exp(g_last - g_cum), jnp.exp(g_last)], axis=-1)
    for d in range(2):
        g_row = lax.dot_general(gs_t, trif[d], NT_DIMS, precision=HIGHEST, preferred_element_type=F32)
        mats = []
        for hh in range(h):
            col = d * h + hh
            diff = g_dir[d][:, col:col + 1] - g_row[col:col + 1, :]
            mats.append(jnp.where(tri[d], jnp.exp(jnp.where(tri[d], diff, 0.0)), 0.0))
        dmat_ref[0, d] = jnp.concatenate(mats, axis=-1)


def _decay_prep(raw, a_log, dt_bias, n_heads):
    b, l, _ = raw.shape
    h = n_heads
    pad = LANES - 2 * h
    alog = jnp.concatenate([a_log.reshape(1, 2 * h), jnp.zeros((1, pad), F32)], axis=-1)
    dtb = jnp.concatenate([dt_bias.reshape(1, 2 * h), jnp.zeros((1, pad), F32)], axis=-1)
    n = l // DN_CHUNK
    return pl.pallas_call(
        functools.partial(_decay_kernel, n_heads=h),
        grid=(b, n),
        in_specs=[pl.BlockSpec((1, DN_CHUNK, LANES), lambda i, j: (i, j, 0)),
                  pl.BlockSpec((1, LANES), lambda i, j: (0, 0)),
                  pl.BlockSpec((1, LANES), lambda i, j: (0, 0))],
        out_specs=[pl.BlockSpec((1, DN_CHUNK, 4 * LANES), lambda i, j: (i, j, 0)),
                   pl.BlockSpec((1, 2, DN_CHUNK, h * DN_CHUNK), lambda i, j: (i, 0, j, 0))],
        out_shape=[jax.ShapeDtypeStruct((b, l, 4 * LANES), F32),
                   jax.ShapeDtypeStruct((b, 2, l, h * DN_CHUNK), F32)],
        compiler_params=_params("parallel", "parallel"),
        name="decay_prep",
    )(raw, alog, dtb)


def _dot16(a, b, dims=None):
    a = a.astype(BF16)
    b = b.astype(BF16)
    if dims is None:
        return jnp.dot(a, b, preferred_element_type=F32)
    return lax.dot_general(a, b, dims, preferred_element_type=F32)


def _unit_triangular_inverses(mats, ii, jj):
    eye = (ii == jj).astype(F32)
    blk8 = (ii // 8) == (jj // 8)
    a0 = [jnp.where(blk8, a, 0.0) for a in mats]
    a2 = [_dot16(x, x) for x in a0]
    a4 = [_dot16(x, x) for x in a2]
    p = [eye - x for x in a0]
    p = [x + _dot16(x, y) for x, y in zip(p, a2)]
    p = [x + _dot16(x, y) for x, y in zip(p, a4)]
    for s in (8, 16, 32):
        sel = ((ii // (2 * s)) == (jj // (2 * s))) & ((ii // s) != (jj // s))
        t = [_dot16(jnp.where(sel, a, 0.0), x) for a, x in zip(mats, p)]
        p = [x - _dot16(x, y) for x, y in zip(p, t)]
    return p


def _delta_kernel(*refs, hb, n_heads, with_q, cps):
    if with_q:
        (kf_ref, vf_ref, qf_ref, kb_ref, vb_ref, qb_ref, df_ref, db_ref, cf_ref, cb_ref, s0_ref,
         of_ref, ob_ref, sfin_ref, s_ref) = refs
        q_refs = (qf_ref, qb_ref)
        o_refs = (of_ref, ob_ref)
    else:
        (kf_ref, vf_ref, kb_ref, vb_ref, df_ref, db_ref, cf_ref, cb_ref, s0_ref, sfin_ref, s_ref) = refs
    k_refs = (kf_ref, kb_ref)
    v_refs = (vf_ref, vb_ref)
    d_refs = (df_ref, db_ref)
    c_refs = (cf_ref, cb_ref)
    hg = pl.program_id(1)
    n = pl.program_id(2)
    c = DN_CHUNK

    @pl.when(n == 0)
    def _():
        s_ref[...] = s0_ref[0]

    ii = lax.broadcasted_iota(I32, (c, c), 0)
    jj = lax.broadcasted_iota(I32, (c, c), 1)
    lane = lax.broadcasted_iota(I32, (c, LANES), 1)

    def column(x, idx):
        return jnp.sum(jnp.where(lane == idx, x, 0.0), axis=-1, keepdims=True)

    chains = [(d, hh) for d in range(2) for hh in range(hb)]
    items = [(d, hh, ci) for ci in range(cps) for d, hh in chains]

    def rows_of(d, ci):
        r0 = (ci if d == 0 else cps - 1 - ci) * c
        return slice(r0, r0 + c)

    beta, eg, ekend, eglast, k, v, dm = [], [], [], [], [], [], []
    for d, hh, ci in items:
        head = hg * hb + hh
        cols = c_refs[d][0][rows_of(d, ci), :]
        beta.append(column(cols[:, 0:LANES], (2 + d) * n_heads + head))
        eg.append(column(cols[:, LANES:2 * LANES], d * n_heads + head))
        ekend.append(column(cols[:, 2 * LANES:3 * LANES], d * n_heads + head))
        eglast.append(column(cols[:, 3 * LANES:4 * LANES], d * n_heads + head)[0:1, :])
        k.append(k_refs[d][0][rows_of(d, ci), hh * HEAD:(hh + 1) * HEAD])
        v.append(v_refs[d][0][rows_of(d, ci), hh * HEAD:(hh + 1) * HEAD])
        dm.append(d_refs[d][0, 0][rows_of(d, ci), hh * c:(hh + 1) * c])
    strict = [(jj < ii) if d == 0 else (jj > ii) for d, _, _ in items]
    kb = [x * y for x, y in zip(k, beta)]
    kk = [_dot16(x, y, NT_DIMS) for x, y in zip(kb, k)]
    a = [jnp.where(m, x * y, 0.0) for m, x, y in zip(strict, kk, dm)]
    t = _unit_triangular_inverses(a, ii, jj)
    u = [_dot16(x, y * z) for x, y, z in zip(t, v, beta)]
    w = [_dot16(x, y * z) for x, y, z in zip(t, kb, eg)]
    ke = [x * y for x, y in zip(k, ekend)]
    if with_q:
        q = [q_refs[d][0][rows_of(d, ci), hh * HEAD:(hh + 1) * HEAD] for d, hh, ci in items]
        qk = [_dot16(x, y, NT_DIMS) * z for x, y, z in zip(q, k, dm)]
        qd = [x * y for x, y in zip(q, eg)]
        o = {}
    s = [s_ref[d, hh] for d, hh in chains]
    for ci in range(cps):
        idx = [ci * len(chains) + n_c for n_c in range(len(chains))]
        v_new = [u[i] - _dot16(w[i], x) for i, x in zip(idx, s)]
        if with_q:
            for i, x, vn, (d, hh) in zip(idx, s, v_new, chains):
                o[(d, hh, ci)] = _dot16(qd[i], x) + _dot16(qk[i], vn)
        upd = [_dot16(ke[i], vn, TN_DIMS) for i, vn in zip(idx, v_new)]
        s = [x * eglast[i] + z for i, x, z in zip(idx, s, upd)]
    for (d, hh), x in zip(chains, s):
        s_ref[d, hh] = x
    if with_q:
        for d in range(2):
            order = range(cps) if d == 0 else range(cps - 1, -1, -1)
            o_refs[d][0] = jnp.concatenate(
                [jnp.concatenate([o[(d, hh, ci)] for hh in range(hb)], axis=-1) for ci in order], axis=0)

    @pl.when(n == pl.num_programs(2) - 1)
    def _():
        sfin_ref[0] = s_ref[...]


def _delta_scan(kvq, cols, dmat, s0, n_heads, with_q):
    b, l, _ = kvq.shape
    h = n_heads
    hb = 4 if h % 4 == 0 else (2 if h % 2 == 0 else 1)
    n_chunks = l // DN_CHUNK
    cps = 4 if n_chunks % 4 == 0 else (2 if n_chunks % 2 == 0 else 1)
    c = cps * DN_CHUNK
    n = l // c
    wblk = h // hb

    def sec(section, rev):
        def imap(bi, g, j):
            return (bi, (n - 1 - j) if rev else j, section * wblk + g)
        return pl.BlockSpec((1, c, hb * HEAD), imap)

    def dspec(d):
        return pl.BlockSpec((1, 1, c, hb * DN_CHUNK), lambda bi, g, j: (bi, d, (n - 1 - j) if d else j, g))

    def cspec(d):
        return pl.BlockSpec((1, c, 4 * LANES), lambda bi, g, j: (bi, (n - 1 - j) if d else j, 0))

    state_spec = pl.BlockSpec((1, 2, hb, HEAD, HEAD), lambda bi, g, j: (bi, 0, g, 0, 0))
    n_sec = 3 if with_q else 2
    in_specs = [sec(s, False) for s in range(n_sec)] + [sec(s, True) for s in range(n_sec)]
    in_specs += [dspec(0), dspec(1), cspec(0), cspec(1), state_spec]
    args = [kvq] * (2 * n_sec) + [dmat, dmat, cols, cols, s0]
    out_specs = [state_spec]
    out_shape = [jax.ShapeDtypeStruct((b, 2, h, HEAD, HEAD), F32)]
    if with_q:
        ospec = [pl.BlockSpec((1, c, hb * HEAD), lambda bi, g, j: (bi, j, g)),
                 pl.BlockSpec((1, c, hb * HEAD), lambda bi, g, j: (bi, n - 1 - j, g))]
        out_specs = ospec + out_specs
        out_shape = [jax.ShapeDtypeStruct((b, l, h * HEAD), F32)] * 2 + out_shape
    res = pl.pallas_call(
        functools.partial(_delta_kernel, hb=hb, n_heads=h, with_q=with_q, cps=cps),
        grid=(b, wblk, n),
        in_specs=in_specs,
        out_specs=out_specs,
        out_shape=out_shape,
        scratch_shapes=[pltpu.VMEM((2, hb, HEAD, HEAD), F32)],
        compiler_params=_params("parallel", "parallel", "arbitrary"),
        name="delta_scan_q" if with_q else "delta_scan_state",
    )(*args)
    if with_q:
        return res[0], res[1], res[2]
    return None, None, res[0]


def _gated_norm_kernel(of_ref, ob_ref, z_ref, g_ref, o_ref):
    o = of_ref[0] + ob_ref[0]
    z = z_ref[0]
    outs = []
    for hh in range(o.shape[-1] // HEAD):
        oh = o[:, hh * HEAD:(hh + 1) * HEAD]
        y = oh * lax.rsqrt(jnp.mean(oh * oh, axis=-1, keepdims=True) + EPS) * g_ref[...]
        outs.append(y * _silu(z[:, hh * HEAD:(hh + 1) * HEAD]))
    o_ref[0] = jnp.concatenate(outs, axis=-1).astype(o_ref.dtype)


def _gated_norm(o_f, o_b, p, z_col, onorm_g):
    b, l, w = o_f.shape
    tl = _tile(l, 512, SUBLANES)
    tc = _tile(w, 512)
    zb = z_col // tc
    return pl.pallas_call(
        _gated_norm_kernel,
        grid=(b, l // tl, w // tc),
        in_specs=[pl.BlockSpec((1, tl, tc), lambda i, j, k: (i, j, k)),
                  pl.BlockSpec((1, tl, tc), lambda i, j, k: (i, j, k)),
                  pl.BlockSpec((1, tl, tc), lambda i, j, k: (i, j, zb + k)),
                  pl.BlockSpec((1, HEAD), lambda i, j, k: (0, 0))],
        out_specs=pl.BlockSpec((1, tl, tc), lambda i, j, k: (i, j, k)),
        out_shape=jax.ShapeDtypeStruct((b, l, w), BF16),
        compiler_params=_params("parallel", "parallel", "parallel"),
        name="gated_norm",
    )(o_f, o_b, p, onorm_g.reshape(1, HEAD))


def _gelu_ln_kernel(x_ref, g_ref, b_ref, o_ref):
    x = _gelu_tanh(x_ref[0])
    mu = jnp.mean(x, axis=-1, keepdims=True)
    xc = x - mu
    var = jnp.mean(xc * xc, axis=-1, keepdims=True)
    o_ref[0] = (xc * lax.rsqrt(var + EPS)) * g_ref[...] + b_ref[...]


def _gelu_layernorm(p, col, width, g, bias):
    b, l, _ = p.shape
    tl = _tile(l, 256, SUBLANES)
    cb = col // width
    return pl.pallas_call(
        _gelu_ln_kernel,
        grid=(b, l // tl),
        in_specs=[pl.BlockSpec((1, tl, width), lambda i, j: (i, j, cb)),
                  pl.BlockSpec((1, width), lambda i, j: (0, 0)),
                  pl.BlockSpec((1, width), lambda i, j: (0, 0))],
        out_specs=pl.BlockSpec((1, tl, width), lambda i, j: (i, j, 0)),
        out_shape=jax.ShapeDtypeStruct((b, l, width), F32),
        compiler_params=_params("parallel", "parallel"),
        name="gelu_layernorm",
    )(p, g.reshape(1, width), bias.reshape(1, width))


def _spatial_kernel(gv_ref, gu_ref, ws_ref, bs_ref, o_ref, *, seq, row_groups, grid_w):
    g = pl.program_id(1)
    w = ws_ref[0].astype(BF16)
    bias = bs_ref[0]
    n_chunks = seq // GM_CHUNK
    rows = seq // grid_w
    cols_per_chunk = GM_CHUNK // rows

    @pl.when(g < row_groups)
    def _():
        for n in range(n_chunks):
            sl = pl.ds(n * GM_CHUNK, GM_CHUNK)
            s = jnp.dot(w, gv_ref[0, sl, :].astype(BF16), preferred_element_type=F32) + bias
            o_ref[0, sl, :] = (_gelu_tanh(gu_ref[0, sl, :]) * s).astype(o_ref.dtype)

    @pl.when(g >= row_groups)
    def _():
        for n in range(n_chunks):
            sls = [pl.ds(n * cols_per_chunk + cc, rows, stride=grid_w) for cc in range(cols_per_chunk)]
            v = jnp.concatenate([gv_ref[0, sl, :] for sl in sls], axis=0)
            s = jnp.dot(w, v.astype(BF16), preferred_element_type=F32) + bias
            for cc, sl in enumerate(sls):
                o_ref[0, sl, :] = (_gelu_tanh(gu_ref[0, sl, :]) * s[cc * rows:(cc + 1) * rows]).astype(o_ref.dtype)


def _spatial_gate(gvn, p, gu_col, gm_ws, gm_bs):
    b, l, width = gvn.shape
    groups = gm_ws.shape[0]
    gub = gu_col // HEAD
    kern = functools.partial(_spatial_kernel, seq=l, row_groups=groups // 2, grid_w=GRID_W)
    return pl.pallas_call(
        kern,
        grid=(b, groups),
        in_specs=[pl.BlockSpec((1, l, HEAD), lambda i, g: (i, 0, g)),
                  pl.BlockSpec((1, l, HEAD), lambda i, g: (i, 0, gub + g)),
                  pl.BlockSpec((1, GM_CHUNK, GM_CHUNK), lambda i, g: (g, 0, 0)),
                  pl.BlockSpec((1, GM_CHUNK, 1), lambda i, g: (g, 0, 0))],
        out_specs=pl.BlockSpec((1, l, HEAD), lambda i, g: (i, 0, g)),
        out_shape=jax.ShapeDtypeStruct((b, l, width), F32),
        compiler_params=_params("parallel", "parallel"),
        name="spatial_gate",
    )(gvn, p, gm_ws, gm_bs.reshape(groups, GM_CHUNK, 1))


def _merge_kernel(ya_ref, yb_ref, wa_ref, wb_ref, ga_ref, gb_ref, o_ref):
    pa = jnp.dot(ya_ref[...], wa_ref[...], preferred_element_type=F32)
    pb = jnp.dot(yb_ref[...].astype(BF16), wb_ref[...], preferred_element_type=F32)
    o_ref[...] = (jax.nn.sigmoid(ga_ref[...]) * pa + jax.nn.sigmoid(gb_ref[...]) * pb).astype(o_ref.dtype)


def _merge(ya, yb, wa, wb, p2d, merge_col):
    m, ka = ya.shape
    kb = yb.shape[1]
    d = wa.shape[1]
    tm = _tile(m, 512, SUBLANES)
    tn = _tile(d, 1024)
    ca = merge_col // tn
    cb = (merge_col + d) // tn
    return pl.pallas_call(
        _merge_kernel,
        grid=(m // tm, d // tn),
        in_specs=[pl.BlockSpec((tm, ka), lambda i, j: (i, 0)),
                  pl.BlockSpec((tm, kb), lambda i, j: (i, 0)),
                  pl.BlockSpec((ka, tn), lambda i, j: (0, j)),
                  pl.BlockSpec((kb, tn), lambda i, j: (0, j)),
                  pl.BlockSpec((tm, tn), lambda i, j: (i, ca + j)),
                  pl.BlockSpec((tm, tn), lambda i, j: (i, cb + j))],
        out_specs=pl.BlockSpec((tm, tn), lambda i, j: (i, j)),
        out_shape=jax.ShapeDtypeStruct((m, d), BF16),
        compiler_params=_params("parallel", "parallel"),
        name="merge",
    )(ya, yb, wa, wb, p2d, p2d)


def _oproj_kernel(a_ref, w_ref, x_ref, gt_ref, o_ref):
    y = jnp.dot(a_ref[...], w_ref[...], preferred_element_type=F32)
    o_ref[...] = x_ref[...] + gt_ref[0] * y


def _out_proj_residual(a, w, x2d, gate, seq):
    m, k = a.shape
    d = w.shape[1]
    tm = _tile(seq, 1024, SUBLANES)
    tn = _tile(d, 1024)
    per_batch = seq // tm
    return pl.pallas_call(
        _oproj_kernel,
        grid=(m // tm, d // tn),
        in_specs=[pl.BlockSpec((tm, k), lambda i, j: (i, 0)),
                  pl.BlockSpec((k, tn), lambda i, j: (0, j)),
                  pl.BlockSpec((tm, tn), lambda i, j: (i, j)),
                  pl.BlockSpec((1, 1, tn), lambda i, j: (i // per_batch, 0, j))],
        out_specs=pl.BlockSpec((tm, tn), lambda i, j: (i, j)),
        out_shape=jax.ShapeDtypeStruct((m, d), F32),
        compiler_params=_params("parallel", "parallel"),
        name="out_proj",
    )(a, w, x2d, gate)


def _router_kernel(x_ref, g_ref, sc_ref, sh_ref, wr_ref, br_ref, h_ref, ti_ref, tw_ref, rk_ref, cnt_ref, carry,
                   *, n_exp, tm):
    i = pl.program_id(0)

    @pl.when(i == 0)
    def _():
        carry[...] = jnp.zeros_like(carry)

    h2 = _rms_mod(x_ref[...], g_ref[...], sc_ref[0], sh_ref[0])
    h_hi = h2.astype(BF16)
    h_lo = (h2 - h_hi.astype(F32)).astype(BF16)
    half = h2.shape[1] // 2
    bits = lax.bitcast_convert_type(h_hi.astype(F32), U32)
    h_ref[...] = bits[:, :half] | (bits[:, half:] >> 16)
    wr = wr_ref[...]
    w_hi = wr.astype(BF16)
    w_lo = (wr - w_hi.astype(F32)).astype(BF16)
    logits = (jnp.dot(h_hi, w_hi, preferred_element_type=F32) + jnp.dot(h_hi, w_lo, preferred_element_type=F32)
              + jnp.dot(h_lo, w_hi, preferred_element_type=F32)) + br_ref[...]
    lane = lax.broadcasted_iota(I32, (tm, n_exp), 1)
    work = logits
    member = jnp.zeros((tm, n_exp), F32)
    vals, ids = [], []
    for _ in range(TOP_K):
        mx = jnp.max(work, axis=-1, keepdims=True)
        idx = jnp.min(jnp.where(work == mx, lane, n_exp), axis=-1, keepdims=True)
        hit = lane == idx
        vals.append(mx)
        ids.append(idx)
        work = jnp.where(hit, -jnp.inf, work)
        member = member + hit.astype(F32)
    exps = [jnp.exp(v - vals[0]) for v in vals]
    den = exps[0]
    for e in exps[1:]:
        den = den + e
    rr = lax.broadcasted_iota(I32, (tm, tm), 0)
    cc = lax.broadcasted_iota(I32, (tm, tm), 1)
    below = (cc < rr).astype(BF16)
    rank_all = jnp.dot(below, member.astype(BF16), preferred_element_type=F32) + carry[...]
    out_lane = lax.broadcasted_iota(I32, (tm, LANES), 1)
    ti = jnp.zeros((tm, LANES), I32)
    tw = jnp.zeros((tm, LANES), F32)
    rk = jnp.zeros((tm, LANES), F32)
    for k in range(TOP_K):
        rank_k = jnp.sum(jnp.where(lane == ids[k], rank_all, 0.0), axis=-1, keepdims=True)
        ti = jnp.where(out_lane == k, ids[k], ti)
        tw = jnp.where(out_lane == k, exps[k] / den, tw)
        rk = jnp.where(out_lane == k, rank_k, rk)
    ti_ref[...] = ti
    tw_ref[...] = tw
    rk_ref[...] = rk.astype(I32)
    carry[...] = carry[...] + jnp.sum(member, axis=0, keepdims=True)
    cnt_ref[...] = carry[...]


def _router(x2d, g, sc, sh, w_router, b_router, seq):
    m, d = x2d.shape
    n_exp = w_router.shape[1]
    tm = _tile(seq, 256, SUBLANES)
    per_batch = seq // tm
    kern = functools.partial(_router_kernel, n_exp=n_exp, tm=tm)
    row = lambda i: (i, 0)
    return pl.pallas_call(
        kern,
        grid=(m // tm,),
        in_specs=[pl.BlockSpec((tm, d), row),
                  pl.BlockSpec((1, d), lambda i: (0, 0)),
                  pl.BlockSpec((1, 1, d), lambda i: (i // per_batch, 0, 0)),
                  pl.BlockSpec((1, 1, d), lambda i: (i // per_batch, 0, 0)),
                  pl.BlockSpec((d, n_exp), lambda i: (0, 0)),
                  pl.BlockSpec((1, n_exp), lambda i: (0, 0))],
        out_specs=[pl.BlockSpec((tm, d // 2), row),
                   pl.BlockSpec((tm, LANES), row),
                   pl.BlockSpec((tm, LANES), row),
                   pl.BlockSpec((tm, LANES), row),
                   pl.BlockSpec((1, n_exp), lambda i: (0, 0))],
        out_shape=[jax.ShapeDtypeStruct((m, d // 2), U32),
                   jax.ShapeDtypeStruct((m, LANES), I32),
                   jax.ShapeDtypeStruct((m, LANES), F32),
                   jax.ShapeDtypeStruct((m, LANES), I32),
                   jax.ShapeDtypeStruct((1, n_exp), F32)],
        scratch_shapes=[pltpu.VMEM((1, n_exp), F32)],
        compiler_params=_params("arbitrary"),
        name="router",
    )(x2d, g.reshape(1, d), sc, sh, w_router, b_router.reshape(1, n_exp))


def _dispatch_kernel(state_ref, tok_ref, nxt_ref, h_hbm, xs_ref, buf, sem, *, rows):
    i = pl.program_id(0)
    slot = i % 2
    live = state_ref[i] != BLK_UNUSED
    nxt = jnp.minimum(i + 1, pl.num_programs(0) - 1)
    next_live = jnp.logical_and(i + 1 < pl.num_programs(0), state_ref[nxt] != BLK_UNUSED)
    group = 8

    def start_block(idx_ref, s):
        def start(g, carry):
            for u in range(group):
                r = g * group + u
                pltpu.make_async_copy(h_hbm.at[pl.ds(idx_ref[r], 1)], buf.at[s, pl.ds(r, 1)],
                                      sem.at[s]).start(priority=u % 2)
            return carry

        lax.fori_loop(0, rows // group, start, 0)

    @pl.when(jnp.logical_and(i == 0, live))
    def _():
        start_block(tok_ref, slot)

    @pl.when(next_live)
    def _():
        start_block(nxt_ref, 1 - slot)

    @pl.when(live)
    def _():
        pltpu.make_async_copy(h_hbm.at[pl.ds(0, rows)], buf.at[slot], sem.at[slot]).wait()
        bits = buf[slot]
        half = bits.shape[1]
        hi = lax.bitcast_convert_type(bits & jnp.uint32(0xFFFF0000), F32)
        lo = lax.bitcast_convert_type(bits << 16, F32)
        xs_ref[:, :half] = hi.astype(xs_ref.dtype)
        xs_ref[:, half:] = lo.astype(xs_ref.dtype)

    @pl.when(jnp.logical_not(live))
    def _():
        xs_ref[...] = jnp.zeros_like(xs_ref)


def _dispatch(h_packed, slot_tok, blk_state):
    n_slots = slot_tok.shape[0]
    half = h_packed.shape[1]
    rows = MOE_ROWS
    n_blk = n_slots // rows
    grid_spec = pltpu.PrefetchScalarGridSpec(
        num_scalar_prefetch=1,
        grid=(n_blk,),
        in_specs=[pl.BlockSpec((rows,), lambda i, st: (i,), memory_space=pltpu.SMEM),
                  pl.BlockSpec((rows,), lambda i, st: (jnp.minimum(i + 1, n_blk - 1),), memory_space=pltpu.SMEM),
                  pl.BlockSpec(memory_space=pl.ANY)],
        out_specs=pl.BlockSpec((rows, 2 * half), lambda i, st: (i, 0)),
        scratch_shapes=[pltpu.VMEM((2, rows, half), U32), pltpu.SemaphoreType.DMA((2,))],
    )
    return pl.pallas_call(
        functools.partial(_dispatch_kernel, rows=rows),
        grid_spec=grid_spec,
        out_shape=jax.ShapeDtypeStruct((n_slots, 2 * half), BF16),
        compiler_params=_params("arbitrary"),
        name="dispatch",
    )(blk_state, slot_tok, slot_tok, h_packed)


def _ffn1_kernel(be_ref, state_ref, xs_ref, wg_ref, wl_ref, bg_ref, bl_ref, o_ref, wg16, wl16):
    del be_ref
    state = state_ref[pl.program_id(1)]

    @pl.when(state == BLK_NEW_EXPERT)
    def _():
        wg16[...] = wg_ref[0].astype(BF16)
        wl16[...] = wl_ref[0].astype(BF16)

    @pl.when(state != BLK_UNUSED)
    def _():
        x = xs_ref[...]
        glu = jnp.dot(x, wg16[...], preferred_element_type=F32) + bg_ref[0]
        lin = jnp.dot(x, wl16[...], preferred_element_type=F32) + bl_ref[0]
        glu = jnp.minimum(glu, SWIGLU_LIMIT)
        lin = jnp.clip(lin, -SWIGLU_LIMIT, SWIGLU_LIMIT)
        o_ref[...] = (glu * jax.nn.sigmoid(SWIGLU_ALPHA * glu) * (lin + 1.0)).astype(o_ref.dtype)

    @pl.when(state == BLK_UNUSED)
    def _():
        o_ref[...] = jnp.zeros_like(o_ref)


def _ffn1(xs, w1, b1, blk_expert, blk_new):
    n_slots, d = xs.shape
    n_exp, _, two_f = w1.shape
    f = two_f // 2
    rows = MOE_ROWS
    tn = _tile(f, 512)
    nj = f // tn
    grid_spec = pltpu.PrefetchScalarGridSpec(
        num_scalar_prefetch=2,
        grid=(nj, n_slots // rows),
        in_specs=[pl.BlockSpec((rows, d), lambda j, i, be, nw: (i, 0)),
                  pl.BlockSpec((1, d, tn), lambda j, i, be, nw: (be[i], 0, j)),
                  pl.BlockSpec((1, d, tn), lambda j, i, be, nw: (be[i], 0, nj + j)),
                  pl.BlockSpec((1, 1, tn), lambda j, i, be, nw: (be[i], 0, j)),
                  pl.BlockSpec((1, 1, tn), lambda j, i, be, nw: (be[i], 0, nj + j))],
        out_specs=pl.BlockSpec((rows, tn), lambda j, i, be, nw: (i, j)),
        scratch_shapes=[pltpu.VMEM((d, tn), BF16), pltpu.VMEM((d, tn), BF16)],
    )
    return pl.pallas_call(
        _ffn1_kernel,
        grid_spec=grid_spec,
        out_shape=jax.ShapeDtypeStruct((n_slots, f), BF16),
        compiler_params=_params("arbitrary", "arbitrary"),
        name="expert_ffn1",
    )(blk_expert, blk_new, xs, w1, w1, b1.reshape(n_exp, 1, two_f), b1.reshape(n_exp, 1, two_f))


def _ffn2_kernel(be_ref, state_ref, a_ref, w_ref, b_ref, o_ref, w16):
    del be_ref
    state = state_ref[pl.program_id(1)]

    @pl.when(state == BLK_NEW_EXPERT)
    def _():
        w16[...] = w_ref[0].astype(BF16)

    @pl.when(state != BLK_UNUSED)
    def _():
        o_ref[...] = jnp.dot(a_ref[...], w16[...], preferred_element_type=F32) + b_ref[0]

    @pl.when(state == BLK_UNUSED)
    def _():
        o_ref[...] = jnp.zeros_like(o_ref)


def _ffn2(act, w2, b2, blk_expert, blk_state):
    n_slots, f = act.shape
    n_exp, _, d = w2.shape
    rows = MOE_ROWS
    tn = _tile(d, 2048)
    grid_spec = pltpu.PrefetchScalarGridSpec(
        num_scalar_prefetch=2,
        grid=(d // tn, n_slots // rows),
        in_specs=[pl.BlockSpec((rows, f), lambda j, i, be, st: (i, 0)),
                  pl.BlockSpec((1, f, tn), lambda j, i, be, st: (be[i], 0, j)),
                  pl.BlockSpec((1, 1, tn), lambda j, i, be, st: (be[i], 0, j))],
        out_specs=pl.BlockSpec((rows, tn), lambda j, i, be, st: (i, j)),
        scratch_shapes=[pltpu.VMEM((f, tn), BF16)],
    )
    return pl.pallas_call(
        _ffn2_kernel,
        grid_spec=grid_spec,
        out_shape=jax.ShapeDtypeStruct((n_slots, d), F32),
        compiler_params=_params("arbitrary", "arbitrary"),
        name="expert_ffn2",
    )(blk_expert, blk_state, act, w2, b2.reshape(n_exp, 1, d))


def _combine_kernel(dest_ref, nxt_ref, x_ref, tw_ref, gt_ref, gf_ref, ys_hbm, o_ref, buf, sem, *, rows):
    i = pl.program_id(0)
    slot = i % 2

    def start_tile(idx_ref, s):
        def start(t, carry):
            for k in range(TOP_K):
                src = ys_hbm.at[pl.ds(idx_ref[t * TOP_K + k], 1)]
                pltpu.make_async_copy(src, buf.at[s, k, pl.ds(t, 1)], sem.at[s]).start(priority=k % 2)
            return carry

        lax.fori_loop(0, rows, start, 0, unroll=2)

    @pl.when(i == 0)
    def _():
        start_tile(dest_ref, slot)

    @pl.when(i + 1 < pl.num_programs(0))
    def _():
        start_tile(nxt_ref, 1 - slot)

    for k in range(TOP_K):
        pltpu.make_async_copy(ys_hbm.at[pl.ds(0, rows)], buf.at[slot, k], sem.at[slot]).wait()
    tw = tw_ref[...]
    y = buf[slot, 0] * tw[:, 0:1]
    for k in range(1, TOP_K):
        y = y + buf[slot, k] * tw[:, k:k + 1]
    x = x_ref[...] + gt_ref[0] * y
    o_ref[...] = x * lax.rsqrt(jnp.mean(x * x, axis=-1, keepdims=True) + EPS) * gf_ref[...]


def _combine(x1, tw, gate, normf_g, ys, dest_flat, seq):
    m, d = x1.shape
    rows = _tile(seq, 128, SUBLANES)
    per_batch = seq // rows
    n_tiles = m // rows
    return pl.pallas_call(
        functools.partial(_combine_kernel, rows=rows),
        grid=(n_tiles,),
        in_specs=[pl.BlockSpec((rows * TOP_K,), lambda i: (i,), memory_space=pltpu.SMEM),
                  pl.BlockSpec((rows * TOP_K,), lambda i: (jnp.minimum(i + 1, n_tiles - 1),), memory_space=pltpu.SMEM),
                  pl.BlockSpec((rows, d), lambda i: (i, 0)),
                  pl.BlockSpec((rows, LANES), lambda i: (i, 0)),
                  pl.BlockSpec((1, 1, d), lambda i: (i // per_batch, 0, 0)),
                  pl.BlockSpec((1, d), lambda i: (0, 0)),
                  pl.BlockSpec(memory_space=pl.ANY)],
        out_specs=pl.BlockSpec((rows, d), lambda i: (i, 0)),
        out_shape=jax.ShapeDtypeStruct((m, d), F32),
        scratch_shapes=[pltpu.VMEM((2, TOP_K, rows, d), F32), pltpu.SemaphoreType.DMA((2,))],
        compiler_params=_params("arbitrary"),
        name="combine",
    )(dest_flat, dest_flat, x1, tw, gate, normf_g.reshape(1, d), ys)


def kernel(x, c, ctx, c_ctx, w_mod, b_mod, norm1_g, w_in, conv_w, a_log, dt_bias, onorm_g, gm_ln_g, gm_ln_b,
           gm_ws, gm_bs, w_up_a, w_up_b, w_o, norm2_g, w_router, b_router, w1, b1, w2, b2, normf_g):
    depth = w_mod.shape[0]
    assert depth == 1, "single-layer block"
    bsz, seq, d = x.shape
    ctx_len = ctx.shape[1]
    n_heads = a_log.shape[-1]
    dn_w = n_heads * HEAD
    gm_w = gm_ln_g.shape[-1]
    n_exp = w_router.shape[-1]
    assert seq % GRID_W == 0 and seq % GM_CHUNK == 0 and seq % DN_CHUNK == 0 and ctx_len % DN_CHUNK == 0
    assert 4 * n_heads <= LANES and GM_CHUNK % (seq // GRID_W) == 0
    (w_mod, b_mod, norm1_g, w_in, conv_w, a_log, dt_bias, onorm_g, gm_ln_g, gm_ln_b, gm_ws, gm_bs, w_up_a, w_up_b,
     w_o, norm2_g, w_router, b_router, w1, b1, w2, b2) = (
        t[0] for t in (w_mod, b_mod, norm1_g, w_in, conv_w, a_log, dt_bias, onorm_g, gm_ln_g, gm_ln_b, gm_ws, gm_bs,
                       w_up_a, w_up_b, w_o, norm2_g, w_router, b_router, w1, b1, w2, b2))

    n_rows = -(-(bsz + 1) // SUBLANES) * SUBLANES
    cond = jnp.concatenate([c, c_ctx[None], jnp.zeros((n_rows - bsz - 1, d), F32)], axis=0)
    mod = _modulation(cond, w_mod, b_mod)
    sh1, sc1, gt1, sh2, sc2, gt2 = (mod[:, i * d:(i + 1) * d].reshape(n_rows, 1, d) for i in range(N_MOD))

    col_decay = 2 * dn_w
    col_q = col_decay + 4 * n_heads
    w_main = jnp.concatenate([w_in[:, :col_decay], w_in[:, col_q:]], axis=1).astype(BF16)
    w_dec = jnp.concatenate([w_in[:, col_decay:col_q], jnp.zeros((d, LANES - 4 * n_heads), F32)], axis=1).astype(BF16)
    col_z = 3 * dn_w
    col_gu = col_z + dn_w
    col_gv = col_gu + gm_w
    col_merge = col_gv + gm_w

    hc = _norm_modulate(ctx, norm1_g, sc1, sh1, lambda i: bsz).reshape(bsz * ctx_len, d)
    pc = _matmul(hc, w_main[:, :2 * dn_w], F32).reshape(bsz, ctx_len, 2 * dn_w)
    rawc = _matmul(hc, w_dec, F32).reshape(bsz, ctx_len, LANES)
    kvc = _short_conv(pc, conv_w, dn_w, 2)
    colsc, dmatc = _decay_prep(rawc, a_log, dt_bias, n_heads)
    zero_state = jnp.zeros((bsz, 2, n_heads, HEAD, HEAD), F32)
    _, _, ctx_state = _delta_scan(kvc, colsc, dmatc, zero_state, n_heads, False)

    h = _norm_modulate(x, norm1_g, sc1, sh1, lambda i: i).reshape(bsz * seq, d)
    p2d = _matmul(h, w_main, F32)
    p = p2d.reshape(bsz, seq, -1)
    raw = _matmul(h, w_dec, F32).reshape(bsz, seq, LANES)
    kvq = _short_conv(p, conv_w, dn_w, 3)
    cols, dmat = _decay_prep(raw, a_log, dt_bias, n_heads)
    o_f, o_b, _ = _delta_scan(kvq, cols, dmat, ctx_state, n_heads, True)
    y_a = _gated_norm(o_f, o_b, p, col_z, onorm_g).reshape(bsz * seq, dn_w)
    gvn = _gelu_layernorm(p, col_gv, gm_w, gm_ln_g, gm_ln_b)
    y_b = _spatial_gate(gvn, p, col_gu, gm_ws, gm_bs).reshape(bsz * seq, gm_w)
    merged = _merge(y_a, y_b, w_up_a.astype(BF16), w_up_b.astype(BF16), p2d, col_merge)
    x2d = x.reshape(bsz * seq, d)
    x1 = _out_proj_residual(merged, w_o.astype(BF16), x2d, gt1, seq)

    h2, ti, tw, rk, cnt = _router(x1, norm2_g, sc2, sh2, w_router, b_router, seq)
    n_tok = bsz * seq
    top_i = ti[:, :TOP_K]
    counts = cnt[0].astype(I32)
    padded = (counts + MOE_ROWS - 1) // MOE_ROWS * MOE_ROWS
    pad_end = jnp.cumsum(padded)
    pad_start = pad_end - padded
    dest = (pad_start[top_i] + rk[:, :TOP_K]).reshape(-1)
    n_slots = n_tok * TOP_K + n_exp * MOE_ROWS
    tok_flat = jnp.repeat(jnp.arange(n_tok, dtype=I32), TOP_K)
    slot_tok = jnp.zeros((n_slots,), I32).at[dest].set(tok_flat)
    blk_start = jnp.arange(n_slots // MOE_ROWS, dtype=I32) * MOE_ROWS
    blk_expert = jnp.minimum(jnp.sum((pad_end[None, :] <= blk_start[:, None]).astype(I32), axis=1), n_exp - 1)
    changed = jnp.concatenate([jnp.ones((1,), jnp.bool_), blk_expert[1:] != blk_expert[:-1]])
    blk_state = jnp.where(blk_start >= pad_end[-1], BLK_UNUSED,
                          jnp.where(changed, BLK_NEW_EXPERT, BLK_SAME_EXPERT)).astype(I32)
    xs = _dispatch(h2, slot_tok, blk_state)
    act = _ffn1(xs, w1, b1, blk_expert, blk_state)
    ys = _ffn2(act, w2, b2, blk_expert, blk_state)
    out = _combine(x1, tw, gt2, normf_g, ys, dest, seq)
    return out.reshape(bsz, seq, d)
```
